```python
import math
import jax, jax.numpy as jnp
from jax import lax
import numpy as np

D_MODEL = 1024
BATCH = 4
SEQ = 4096
DEPTH = 4
DEC_BATCH = 32
DEC_SEQ = 1
PAST_LEN = 8192
PAGE_SIZE = 128

N_EVEN = (DEPTH + 1) // 2
N_ODD = DEPTH // 2
D_FF = 2816
EPS = 1e-6
CONV_CH = D_MODEL // 2
CONV_W = 31
SB_HEADS = 8
SB_HEAD_DIM = (D_MODEL // 2) // SB_HEADS
SB_QBLOCK = 128
SB_BIAS_INIT = -8.0
M_D_INNER = 2 * D_MODEL
M_HEAD_DIM = 64
M_HEADS = M_D_INNER // M_HEAD_DIM
M_GROUPS = 4
M_HPG = M_HEADS // M_GROUPS
M_D_STATE = 128
M_CONV_W = 4
M_CONV_DIM = M_D_INNER + 2 * M_GROUPS * M_D_STATE
M_CHUNK = 128
M_IN = M_D_INNER + M_CONV_DIM + M_HEADS
AB_IN = 2 * CONV_CH + 3 * SB_HEADS * SB_HEAD_DIM
AB_MIX = CONV_CH + SB_HEADS * SB_HEAD_DIM

kernel_name = 'stickbreak_conformer_mamba2_hybrid_step'

F32 = jnp.float32


def rmsnorm(x, w):
    xf = x.astype(F32)
    y = xf * lax.rsqrt(jnp.mean(xf * xf, axis=-1, keepdims=True) + EPS)
    return (y * w.astype(F32)).astype(x.dtype)


def swiglu_ffn(x, norm_w, w_gate, w_up, w_down):
    h = rmsnorm(x, norm_w)
    return (jax.nn.silu(h @ w_gate) * (h @ w_up)) @ w_down


def causal_dwconv(u, prev, w, b):
    up = jnp.concatenate([prev.astype(u.dtype), u], axis=1)
    out = lax.conv_general_dilated(up, w.astype(u.dtype)[:, None, :], window_strides=(1,), padding='VALID',
                                   dimension_numbers=('NWC', 'WIO', 'NWC'), feature_group_count=u.shape[-1])
    new_state = up[:, up.shape[1] - (w.shape[0] - 1):]
    return out + b.astype(u.dtype), new_state


def sb_block(q, k, v, q_pos, k_pos, bias):
    z = jnp.einsum('bqhd,bkhd->bhqk', q.astype(F32), k.astype(F32)) * (SB_HEAD_DIM ** -0.5)
    z = z + bias.astype(F32)[None, :, None, None]
    mask = k_pos[None, :] < q_pos[:, None]
    lb = jnp.where(mask, jax.nn.log_sigmoid(-z), 0.0)
    rev = lax.cumsum(lb, axis=3, reverse=True)
    after = jnp.concatenate([rev[..., 1:], jnp.zeros_like(rev[..., :1])], axis=-1)
    wgt = jnp.where(mask, jnp.exp(jax.nn.log_sigmoid(z) + after), 0.0)
    return jnp.einsum('bhqk,bkhd->bqhd', wgt, v.astype(F32)).astype(v.dtype)


def sb_attention(q, k, v, q_offset, bias):
    b, tq = q.shape[0], q.shape[1]
    k_pos = jnp.arange(k.shape[1], dtype=jnp.int32)
    if tq % SB_QBLOCK != 0 or tq <= SB_QBLOCK:
        q_pos = q_offset + jnp.arange(tq, dtype=jnp.int32)
        return sb_block(q, k, v, q_pos, k_pos, bias)
    nb = tq // SB_QBLOCK

    def one_block(i):
        start = i * SB_QBLOCK
        qb = lax.dynamic_slice_in_dim(q, start, SB_QBLOCK, axis=1)
        q_pos = q_offset + start + jnp.arange(SB_QBLOCK, dtype=jnp.int32)
        return sb_block(qb, k, v, q_pos, k_pos, bias)

    out = lax.map(one_block, jnp.arange(nb, dtype=jnp.int32))
    return jnp.moveaxis(out, 0, 1).reshape(b, tq, SB_HEADS, SB_HEAD_DIM)


def conv_attn_mixer(h, conv_prev, k_past, v_past, q_offset, w_in, conv_w, conv_b, ln_g, ln_b, sb_bias, w_out):
    b, t, _ = h.shape
    proj = h @ w_in
    hd = SB_HEADS * SB_HEAD_DIM
    glu_a = proj[..., :CONV_CH]
    glu_g = proj[..., CONV_CH:2 * CONV_CH]
    q = proj[..., 2 * CONV_CH:2 * CONV_CH + hd].reshape(b, t, SB_HEADS, SB_HEAD_DIM)
    k = proj[..., 2 * CONV_CH + hd:2 * CONV_CH + 2 * hd].reshape(b, t, SB_HEADS, SB_HEAD_DIM)
    v = proj[..., 2 * CONV_CH + 2 * hd:].reshape(b, t, SB_HEADS, SB_HEAD_DIM)
    u = glu_a * jax.nn.sigmoid(glu_g)
    c, conv_new = causal_dwconv(u, conv_prev, conv_w, conv_b)
    cf = c.astype(F32)
    mu = jnp.mean(cf, axis=-1, keepdims=True)
    var = jnp.mean(jnp.square(cf - mu), axis=-1, keepdims=True)
    cn = (cf - mu) * lax.rsqrt(var + EPS) * ln_g.astype(F32) + ln_b.astype(F32)
    a_out = jax.nn.silu(cn).astype(h.dtype)
    if k_past is None:
        k_all, v_all = k, v
    else:
        k_all = jnp.concatenate([k_past.astype(k.dtype), k], axis=1)
        v_all = jnp.concatenate([v_past.astype(v.dtype), v], axis=1)
    o = sb_attention(q, k_all, v_all, q_offset, sb_bias).reshape(b, t, hd)
    mixed = jnp.concatenate([a_out, o], axis=-1)
    return mixed @ w_out, conv_new, k, v


def ssd_scan(x, dt, a, bmat, cmat, h0, chunk):
    b, t, g, hg, p = x.shape
    n = bmat.shape[-1]
    nc = t // chunk
    xf = x.astype(F32).reshape(b, nc, chunk, g, hg, p)
    dtc = dt.reshape(b, nc, chunk, g, hg)
    bf = bmat.astype(F32).reshape(b, nc, chunk, g, n)
    cf = cmat.astype(F32).reshape(b, nc, chunk, g, n)
    a_cs = jnp.cumsum(dtc * a, axis=2)
    dtx = xf * dtc[..., None]
    cb = jnp.einsum('bclgn,bcsgn->bcgls', cf, bf)
    acs = jnp.moveaxis(a_cs, 2, -1)
    causal = jnp.tril(jnp.ones((chunk, chunk), dtype=bool))
    decay = jnp.exp(jnp.where(causal, acs[..., :, None] - acs[..., None, :], -jnp.inf))
    y_diag = jnp.einsum('bcghls,bcsghp->bclghp', cb[:, :, :, None] * decay, dtx)
    dtx_end = dtx * jnp.exp(a_cs[:, :, -1:] - a_cs)[..., None]
    s_chunk = jnp.einsum('bcsgn,bcsghp->bcghpn', bf, dtx_end)
    a_tot = a_cs[:, :, -1]

    def step(hc, inp):
        at, sc = inp
        return jnp.exp(at)[..., None, None] * hc + sc, hc

    h_last, h_prev = lax.scan(step, h0.astype(F32), (jnp.moveaxis(a_tot, 1, 0), jnp.moveaxis(s_chunk, 1, 0)))
    h_prev = jnp.moveaxis(h_prev, 0, 1)
    y_off = jnp.einsum('bclgn,bcghpn->bclghp', cf, h_prev) * jnp.exp(a_cs)[..., None]
    return (y_diag + y_off).reshape(b, t, g, hg, p), h_last


def mamba2_mixer(h, conv_prev, ssm_prev, w_in, conv_w, conv_b, dt_bias, a_log, d_skip, norm_w, w_out):
    b, t, _ = h.shape
    proj = h @ w_in
    z = proj[..., :M_D_INNER]
    xbc = proj[..., M_D_INNER:M_D_INNER + M_CONV_DIM]
    dt_raw = proj[..., M_D_INNER + M_CONV_DIM:]
    xbc, conv_new = causal_dwconv(xbc, conv_prev, conv_w, conv_b)
    xbc = jax.nn.silu(xbc)
    gn = M_GROUPS * M_D_STATE
    x = xbc[..., :M_D_INNER].reshape(b, t, M_GROUPS, M_HPG, M_HEAD_DIM)
    bmat = xbc[..., M_D_INNER:M_D_INNER + gn].reshape(b, t, M_GROUPS, M_D_STATE)
    cmat = xbc[..., M_D_INNER + gn:].reshape(b, t, M_GROUPS, M_D_STATE)
    dt = jax.nn.softplus(dt_raw.astype(F32) + dt_bias.astype(F32)).reshape(b, t, M_GROUPS, M_HPG)
    a = -jnp.exp(a_log.astype(F32)).reshape(M_GROUPS, M_HPG)
    chunk = M_CHUNK if t % M_CHUNK == 0 else t
    h0 = ssm_prev.astype(F32).reshape(b, M_GROUPS, M_HPG, M_HEAD_DIM, M_D_STATE)
    y, h_last = ssd_scan(x, dt, a, bmat, cmat, h0, chunk)
    y = y + d_skip.astype(F32).reshape(M_GROUPS, M_HPG)[:, :, None] * x.astype(F32)
    y = y.reshape(b, t, M_D_INNER) * jax.nn.silu(z.astype(F32))
    yg = y.reshape(b, t, M_GROUPS, M_D_INNER // M_GROUPS)
    yg = yg * lax.rsqrt(jnp.mean(yg * yg, axis=-1, keepdims=True) + EPS)
    y = yg.reshape(b, t, M_D_INNER) * norm_w.astype(F32)
    out = y.astype(h.dtype) @ w_out
    return out, conv_new, h_last.reshape(b, M_HEADS, M_HEAD_DIM, M_D_STATE)


def setup_inputs(seed: int = 0) -> dict:
    key = jax.random.key(seed)
    ks = iter(jax.random.split(key, 48))

    def nrm(shape, scale):
        return jax.random.normal(next(ks), shape, F32) * scale

    n_pages = PAST_LEN // PAGE_SIZE
    n_used = DEC_BATCH * n_pages
    n_phys = n_used + n_used // 4
    perm = jax.random.permutation(next(ks), n_phys)
    page_table = perm[:n_used].reshape(DEC_BATCH, n_pages).astype(jnp.int32)

    dt0 = jnp.exp(jax.random.uniform(next(ks), (N_ODD, M_HEADS), F32, math.log(1e-3), math.log(1e-1)))
    dt_bias = dt0 + jnp.log(-jnp.expm1(-dt0))
    a_log = jnp.log(jax.random.uniform(next(ks), (N_ODD, M_HEADS), F32, 1.0, 16.0))

    return {
        'x_prompt': nrm((BATCH, SEQ, D_MODEL), 1.0),
        'x_sample': nrm((DEC_BATCH, DEC_SEQ, D_MODEL), 1.0),
        'cache_sb_k': nrm((N_EVEN, n_phys, PAGE_SIZE, SB_HEADS, SB_HEAD_DIM), 1.0),
        'cache_sb_v': nrm((N_EVEN, n_phys, PAGE_SIZE, SB_HEADS, SB_HEAD_DIM), 1.0),
        'state_conv_a': nrm((N_EVEN, DEC_BATCH, CONV_W - 1, CONV_CH), 0.5),
        'state_mamba_conv': nrm((N_ODD, DEC_BATCH, M_CONV_W - 1, M_CONV_DIM), 1.0),
        'state_ssm': nrm((N_ODD, DEC_BATCH, M_HEADS, M_HEAD_DIM, M_D_STATE), 0.1),
        'page_table': page_table,
        'ffn1_norm': 1.0 + nrm((DEPTH, D_MODEL), 0.02),
        'ffn1_gate': nrm((DEPTH, D_MODEL, D_FF), D_MODEL ** -0.5),
        'ffn1_up': nrm((DEPTH, D_MODEL, D_FF), D_MODEL ** -0.5),
        'ffn1_down': nrm((DEPTH, D_FF, D_MODEL), D_FF ** -0.5),
        'mix_norm': 1.0 + nrm((DEPTH, D_MODEL), 0.02),
        'ffn2_norm': 1.0 + nrm((DEPTH, D_MODEL), 0.02),
        'ffn2_gate': nrm((DEPTH, D_MODEL, D_FF), D_MODEL ** -0.5),
        'ffn2_up': nrm((DEPTH, D_MODEL, D_FF), D_MODEL ** -0.5),
        'ffn2_down': nrm((DEPTH, D_FF, D_MODEL), D_FF ** -0.5),
        'final_norm': 1.0 + nrm((D_MODEL,), 0.02),
        'ab_in_proj': nrm((N_EVEN, D_MODEL, AB_IN), D_MODEL ** -0.5),
        'conv_a_w': nrm((N_EVEN, CONV_W, CONV_CH), CONV_W ** -0.5),
        'conv_a_b': nrm((N_EVEN, CONV_CH), 0.02),
        'conv_a_ln_g': 1.0 + nrm((N_EVEN, CONV_CH), 0.02),
        'conv_a_ln_b': nrm((N_EVEN, CONV_CH), 0.02),
        'sb_beta_bias': SB_BIAS_INIT + nrm((N_EVEN, SB_HEADS), 0.5),
        'ab_out_proj': nrm((N_EVEN, AB_MIX, D_MODEL), AB_MIX ** -0.5),
        'm_in_proj': nrm((N_ODD, D_MODEL, M_IN), D_MODEL ** -0.5),
        'm_conv_w': nrm((N_ODD, M_CONV_W, M_CONV_DIM), M_CONV_W ** -0.5),
        'm_conv_b': nrm((N_ODD, M_CONV_DIM), 0.02),
        'm_dt_bias': dt_bias,
        'm_A_log': a_log,
        'm_D': 1.0 + nrm((N_ODD, M_HEADS), 0.1),
        'm_norm': 1.0 + nrm((N_ODD, M_D_INNER), 0.02),
        'm_out_proj': nrm((N_ODD, M_D_INNER, D_MODEL), M_D_INNER ** -0.5),
    }


def reference(x_prompt, x_sample, cache_sb_k, cache_sb_v, state_conv_a, state_mamba_conv, state_ssm, page_table,
              ffn1_norm, ffn1_gate, ffn1_up, ffn1_down, mix_norm, ffn2_norm, ffn2_gate, ffn2_up, ffn2_down,
              final_norm, ab_in_proj, conv_a_w, conv_a_b, conv_a_ln_g, conv_a_ln_b, sb_beta_bias, ab_out_proj,
              m_in_proj, m_conv_w, m_conv_b, m_dt_bias, m_A_log, m_D, m_norm, m_out_proj):
    dec_b, n_pages = page_table.shape
    past_len = n_pages * cache_sb_k.shape[2]
    bp = x_prompt.shape[0]
    xp, xs = x_prompt, x_sample
    kp_l, vp_l, cap_l, mcp_l, ssp_l = [], [], [], [], []
    ks_l, vs_l, cas_l, mcs_l, sss_l = [], [], [], [], []
    for layer in range(DEPTH):
        f1 = (ffn1_norm[layer], ffn1_gate[layer], ffn1_up[layer], ffn1_down[layer])
        xp = xp + 0.5 * swiglu_ffn(xp, *f1)
        xs = xs + 0.5 * swiglu_ffn(xs, *f1)
        hp = rmsnorm(xp, mix_norm[layer])
        hs = rmsnorm(xs, mix_norm[layer])
        if layer % 2 == 0:
            e = layer // 2
            wa = (ab_in_proj[e], conv_a_w[e], conv_a_b[e], conv_a_ln_g[e], conv_a_ln_b[e], sb_beta_bias[e],
                  ab_out_proj[e])
            conv0 = jnp.zeros((bp, CONV_W - 1, CONV_CH), xp.dtype)
            mp, cvp, kp, vp = conv_attn_mixer(hp, conv0, None, None, 0, *wa)
            k_past = cache_sb_k[e][page_table].reshape(dec_b, past_len, SB_HEADS, SB_HEAD_DIM)
            v_past = cache_sb_v[e][page_table].reshape(dec_b, past_len, SB_HEADS, SB_HEAD_DIM)
            ms, cvs, ks_new, vs_new = conv_attn_mixer(hs, state_conv_a[e], k_past, v_past, past_len, *wa)
            kp_l.append(kp); vp_l.append(vp); cap_l.append(cvp)
            ks_l.append(ks_new); vs_l.append(vs_new); cas_l.append(cvs)
        else:
            o = layer // 2
            wm = (m_in_proj[o], m_conv_w[o], m_conv_b[o], m_dt_bias[o], m_A_log[o], m_D[o], m_norm[o], m_out_proj[o])
            mconv0 = jnp.zeros((bp, M_CONV_W - 1, M_CONV_DIM), xp.dtype)
            ssm0 = jnp.zeros((bp, M_HEADS, M_HEAD_DIM, M_D_STATE), F32)
            mp, mcp, ssp = mamba2_mixer(hp, mconv0, ssm0, *wm)
            ms, mcs, sss = mamba2_mixer(hs, state_mamba_conv[o], state_ssm[o], *wm)
            mcp_l.append(mcp); ssp_l.append(ssp)
            mcs_l.append(mcs); sss_l.append(sss)
        xp = xp + mp
        xs = xs + ms
        f2 = (ffn2_norm[layer], ffn2_gate[layer], ffn2_up[layer], ffn2_down[layer])
        xp = xp + 0.5 * swiglu_ffn(xp, *f2)
        xs = xs + 0.5 * swiglu_ffn(xs, *f2)
    y_prompt = rmsnorm(xp, final_norm)
    y_sample = rmsnorm(xs, final_norm)
    return (y_prompt, y_sample,
            jnp.stack(kp_l), jnp.stack(vp_l), jnp.stack(cap_l), jnp.stack(mcp_l), jnp.stack(ssp_l),
            jnp.stack(ks_l), jnp.stack(vs_l), jnp.stack(cas_l), jnp.stack(mcs_l), jnp.stack(sss_l))
```

```python
import functools

import jax
import jax.numpy as jnp
from jax import lax
from jax.experimental import pallas as pl
from jax.experimental.pallas import tpu as pltpu

F32 = jnp.float32
BF16 = jnp.bfloat16
EPS = 1e-6

LANES = 128
SUBLANES = 8
VMEM_LIMIT_BYTES = 56 * 1024 * 1024

SB_HEADS = 8
SB_HEAD_DIM = 64
CONV_W = 31
M_HEAD_DIM = 64
M_GROUPS = 4
M_D_STATE = 128
M_CONV_W = 4
SSD_CHUNK = 128


def _cparams(*sem):
    return pltpu.CompilerParams(dimension_semantics=sem, vmem_limit_bytes=VMEM_LIMIT_BYTES)


def _resident(shape):
    nd = len(shape)
    return pl.BlockSpec(shape, lambda *_: (0,) * nd, pipeline_mode=pl.Buffered(1))


def _resident_layer(shape, layer):
    nd = len(shape)
    return pl.BlockSpec((None,) + tuple(shape[1:]), lambda *_: (layer,) + (0,) * (nd - 1),
                        pipeline_mode=pl.Buffered(1))


def _rms_bf16(x, w):
    y = x * lax.rsqrt(jnp.mean(x * x, axis=-1, keepdims=True) + EPS)
    return (y * w).astype(BF16)


def _dot(a, b):
    return jnp.dot(a, b, preferred_element_type=F32)


def _dot_nt(a, b):
    return lax.dot_general(a, b, (((1,), (1,)), ((), ())), preferred_element_type=F32)


def _split2(v):
    hi = v.astype(BF16)
    lo = (v - hi.astype(F32)).astype(BF16)
    return hi, lo


def _split3(v):
    hi = v.astype(BF16)
    r = v - hi.astype(F32)
    mid = r.astype(BF16)
    lo = (r - mid.astype(F32)).astype(BF16)
    return hi, mid, lo


def _dot_exact_rhs(parts, m):
    acc = _dot(parts[0], m)
    for p in parts[1:]:
        acc = acc + _dot(p, m)
    return acc


def _softplus(z):
    return jnp.maximum(z, 0.0) + jnp.log1p(jnp.exp(-jnp.abs(z)))


def _silu(v):
    return v * jax.nn.sigmoid(v)


def _token_tile(t, want):
    return want if t % want == 0 else t


def _ffn_body(x_ref, nw_ref, wg_ref, wu_ref, wd_ref, fw_ref, o_ref, *, final):
    x = x_ref[...]
    h = _rms_bf16(x, nw_ref[...])
    g = _dot(h, wg_ref[...])
    u = _dot(h, wu_ref[...])
    a = (_silu(g) * u).astype(BF16)
    y = x + 0.5 * _dot(a, wd_ref[...])
    if final:
        y = y * lax.rsqrt(jnp.mean(y * y, axis=-1, keepdims=True) + EPS) * fw_ref[...]
    o_ref[...] = y


def _ffn(x, nw, wg, wu, wd, fw, layer, *, final=False):
    t, d = x.shape
    tm = _token_tile(t, 512)
    return pl.pallas_call(
        functools.partial(_ffn_body, final=final),
        grid=(t // tm,),
        in_specs=[pl.BlockSpec((tm, d), lambda i: (i, 0)),
                  _resident_layer(nw.shape, layer), _resident_layer(wg.shape, layer),
                  _resident_layer(wu.shape, layer), _resident_layer(wd.shape, layer),
                  _resident((1, d))],
        out_specs=pl.BlockSpec((tm, d), lambda i: (i, 0)),
        out_shape=jax.ShapeDtypeStruct((t, d), F32),
        compiler_params=_cparams("parallel"),
        name="ffn",
    )(x, nw, wg, wu, wd, fw)


def _ab_in_body(x_ref, nw_ref, w_ref, u_ref, qb_ref, k_ref, v_ref, kb_ref, vb_ref, *, c, hd):
    h = _rms_bf16(x_ref[...], nw_ref[...])
    p = _dot(h, w_ref[...])
    u_ref[...] = p[:, :c] * jax.nn.sigmoid(p[:, c:2 * c])
    qb_ref[...] = (p[:, 2 * c:2 * c + hd] * (SB_HEAD_DIM ** -0.5)).astype(BF16)
    k = p[:, 2 * c + hd:2 * c + 2 * hd]
    v = p[:, 2 * c + 2 * hd:]
    k_ref[...] = k
    v_ref[...] = v
    kb_ref[...] = k.astype(BF16)
    vb_ref[...] = v.astype(BF16)


def _ab_in(x, nw, w):
    t, d = x.shape
    hd = SB_HEADS * SB_HEAD_DIM
    c = (w.shape[1] - 3 * hd) // 2
    tm = _token_tile(t, 512)
    row = lambda n: pl.BlockSpec((tm, n), lambda i: (i, 0))
    return pl.pallas_call(
        functools.partial(_ab_in_body, c=c, hd=hd),
        grid=(t // tm,),
        in_specs=[row(d), _resident((1, d)), _resident(w.shape)],
        out_specs=[row(c), row(hd), row(hd), row(hd), row(hd), row(hd)],
        out_shape=[jax.ShapeDtypeStruct((t, c), F32), jax.ShapeDtypeStruct((t, hd), BF16),
                   jax.ShapeDtypeStruct((t, hd), F32), jax.ShapeDtypeStruct((t, hd), F32),
                   jax.ShapeDtypeStruct((t, hd), BF16), jax.ShapeDtypeStruct((t, hd), BF16)],
        compiler_params=_cparams("parallel"),
        name="ab_in",
    )(x, nw, w)


_CONV_ROWS = 32
_CONV_HALO = 32


def _layernorm_silu(cv, g, b):
    mu = jnp.mean(cv, axis=-1, keepdims=True)
    d = cv - mu
    var = jnp.mean(d * d, axis=-1, keepdims=True)
    return _silu(d * lax.rsqrt(var + EPS) * g + b)


def _conv_a_body(u_ref, w_ref, b_ref, g_ref, be_ref, a_ref, cn_ref, ext_ref, *, tt):
    t = pl.program_id(1)

    @pl.when(t == 0)
    def _():
        ext_ref[0:_CONV_HALO, :] = jnp.zeros((_CONV_HALO, ext_ref.shape[1]), F32)

    ext_ref[_CONV_HALO:_CONV_HALO + tt, :] = u_ref[...]
    first = _CONV_HALO - (CONV_W - 1)
    for r in range(tt // _CONV_ROWS):
        base = r * _CONV_ROWS + first
        acc = w_ref[0:1, :] * ext_ref[base:base + _CONV_ROWS, :]
        for j in range(1, CONV_W):
            acc = acc + w_ref[j:j + 1, :] * ext_ref[base + j:base + j + _CONV_ROWS, :]
        cv = acc + b_ref[...]
        a_ref[r * _CONV_ROWS:(r + 1) * _CONV_ROWS, :] = _layernorm_silu(cv, g_ref[...], be_ref[...]).astype(BF16)

    @pl.when(t == pl.num_programs(1) - 1)
    def _():
        cn_ref[...] = ext_ref[_CONV_HALO + tt - (CONV_W - 1):_CONV_HALO + tt, :]

    ext_ref[0:_CONV_HALO, :] = ext_ref[tt:tt + _CONV_HALO, :]


def _conv_a_prompt(u, bsz, w, b, g, be):
    t_all, c = u.shape
    t = t_all // bsz
    tt = _token_tile(t, 512)
    nt = t // tt
    return pl.pallas_call(
        functools.partial(_conv_a_body, tt=tt),
        grid=(bsz, nt),
        in_specs=[pl.BlockSpec((tt, c), lambda i, j: (i * nt + j, 0)),
                  _resident(w.shape), _resident((1, c)), _resident((1, c)), _resident((1, c))],
        out_specs=[pl.BlockSpec((tt, c), lambda i, j: (i * nt + j, 0)),
                   pl.BlockSpec((None, CONV_W - 1, c), lambda i, j: (i, 0, 0))],
        out_shape=[jax.ShapeDtypeStruct((t_all, c), BF16),
                   jax.ShapeDtypeStruct((bsz, CONV_W - 1, c), F32)],
        scratch_shapes=[pltpu.VMEM((_CONV_HALO + tt, c), F32)],
        compiler_params=_cparams("parallel", "arbitrary"),
        name="conv_a_prompt",
    )(u, w, b, g, be)


def _conv_a_sample_body(prev_ref, u_ref, w_ref, b_ref, g_ref, be_ref, a_ref, cn_ref):
    nprev = CONV_W - 1
    prev = prev_ref[...]
    u = u_ref[...]
    cv = jnp.sum(w_ref[0:nprev, :] * prev, axis=0, keepdims=True) + w_ref[nprev:CONV_W, :] * u + b_ref[...]
    a_ref[...] = _layernorm_silu(cv, g_ref[...], be_ref[...])
    cn_ref[0:nprev - 1, :] = prev_ref[1:nprev, :]
    cn_ref[nprev - 1:nprev, :] = u


def _conv_a_sample(prev, u, w, b, g, be):
    bsz, nprev, c = prev.shape
    return pl.pallas_call(
        _conv_a_sample_body,
        grid=(bsz,),
        in_specs=[pl.BlockSpec((None, nprev, c), lambda i: (i, 0, 0)),
                  pl.BlockSpec((None, 1, c), lambda i: (i, 0, 0)),
                  _resident(w.shape), _resident((1, c)), _resident((1, c)), _resident((1, c))],
        out_specs=[pl.BlockSpec((None, 1, c), lambda i: (i, 0, 0)),
                   pl.BlockSpec((None, nprev, c), lambda i: (i, 0, 0))],
        out_shape=[jax.ShapeDtypeStruct((bsz, 1, c), F32),
                   jax.ShapeDtypeStruct((bsz, nprev, c), F32)],
        compiler_params=_cparams("parallel"),
        name="conv_a_sample",
    )(prev, u.reshape(bsz, 1, c), w, b, g, be)


def _suffix_matrix(tk):
    j = lax.broadcasted_iota(jnp.int32, (tk, tk + LANES), 0)
    s = lax.broadcasted_iota(jnp.int32, (tk, tk + LANES), 1)
    return jnp.where((j > s) | (s >= tk), 1.0, 0.0).astype(BF16)


def _sb_block(z, v_masked, sufm, carry, valid):
    tk = z.shape[1]
    sp = _softplus(z)
    arg = z - sp
    if valid is not None:
        sp = jnp.where(valid, sp, 0.0)
    sums = _dot_exact_rhs(_split2(sp), sufm)
    after = jnp.concatenate([sums[:, c * LANES:(c + 1) * LANES] + carry for c in range(tk // LANES)], axis=1)
    wgt = jnp.exp(arg - after)
    if valid is not None:
        wgt = jnp.where(valid, wgt, 0.0)
    out = _dot(wgt.astype(BF16), v_masked)
    return out, carry + sums[:, tk:]


def _sb_prompt_body(bias_ref, q_ref, k_ref, v_ref, o_ref, *, tq):
    i = pl.program_id(1)
    tk = tq
    sufm = _suffix_matrix(tk)
    lane = lax.broadcasted_iota(jnp.int32, (1, LANES), 1)
    row = lax.broadcasted_iota(jnp.int32, (tq, tk), 0)
    col = lax.broadcasted_iota(jnp.int32, (tq, tk), 1)
    diag_valid = col < row
    for hp in range(SB_HEADS // 2):
        lanes = slice(hp * LANES, (hp + 1) * LANES)
        qp = q_ref[:, lanes]
        acc = jnp.zeros((tq, LANES), F32)
        for hh in range(2):
            head = 2 * hp + hh
            sel = (lane >= hh * SB_HEAD_DIM) & (lane < (hh + 1) * SB_HEAD_DIM)
            qm = jnp.where(sel, qp, jnp.zeros_like(qp))
            bias = bias_ref[head]

            def block(kj, carry, valid, qm=qm, sel=sel, bias=bias, lanes=lanes):
                start = pl.multiple_of(kj * tk, tk)
                kb = k_ref[pl.ds(start, tk), lanes]
                vb = v_ref[pl.ds(start, tk), lanes]
                vm = jnp.where(sel, vb, jnp.zeros_like(vb))
                z = _dot_nt(qm, kb) + bias
                return _sb_block(z, vm, sufm, carry, valid)

            out, carry = block(i, jnp.zeros((tq, LANES), F32), diag_valid)

            def step(jj, st):
                acc_h, carry_h = st
                o, c2 = block(i - jj, carry_h, None)
                return acc_h + o, c2

            out, _ = lax.fori_loop(1, i + 1, step, (out, carry))
            acc = acc + out
        o_ref[:, lanes] = acc.astype(o_ref.dtype)


def _sb_attn_prompt(qb, kb, vb, bias, bsz):
    t_all, hd = qb.shape
    t = t_all // bsz
    tq = _token_tile(t, 256)
    nq = t // tq
    return pl.pallas_call(
        functools.partial(_sb_prompt_body, tq=tq),
        grid=(bsz, nq),
        in_specs=[pl.BlockSpec(memory_space=pltpu.SMEM),
                  pl.BlockSpec((tq, hd), lambda b, i: (b * nq + i, 0)),
                  pl.BlockSpec((t, hd), lambda b, i: (b, 0)),
                  pl.BlockSpec((t, hd), lambda b, i: (b, 0))],
        out_specs=pl.BlockSpec((tq, hd), lambda b, i: (b * nq + i, 0)),
        out_shape=jax.ShapeDtypeStruct((t_all, hd), BF16),
        compiler_params=_cparams("parallel", "arbitrary"),
        name="sb_attn_prompt",
    )(bias, qb, kb, vb)


def _sb_decode_body(pt_ref, bias_ref, q_ref, k_ref, v_ref, o_ref, acc_ref, carry_ref, *, page):
    p = pl.program_id(1)
    hd = SB_HEADS * SB_HEAD_DIM

    @pl.when(p == 0)
    def _():
        acc_ref[...] = jnp.zeros_like(acc_ref)
        carry_ref[...] = jnp.zeros_like(carry_ref)

    head_of_lane = lax.broadcasted_iota(jnp.int32, (SB_HEADS, hd), 1) // SB_HEAD_DIM
    head_of_row = lax.broadcasted_iota(jnp.int32, (SB_HEADS, hd), 0)
    own = head_of_lane == head_of_row
    q8 = jnp.where(own, jnp.broadcast_to(q_ref[...], (SB_HEADS, hd)), 0.0).astype(BF16)
    z = _dot_nt(q8, k_ref[...].astype(BF16)) + bias_ref[...]
    sp = _softplus(z)
    sums = _dot_exact_rhs(_split2(sp), _suffix_matrix(page))
    wgt = jnp.exp(z - sp - (sums[:, :page] + carry_ref[...]))
    acc_ref[...] += _dot(wgt.astype(BF16), v_ref[...].astype(BF16))
    carry_ref[...] += sums[:, page:]

    @pl.when(p == pl.num_programs(1) - 1)
    def _():
        o_ref[...] = jnp.sum(jnp.where(own, acc_ref[...], 0.0), axis=0, keepdims=True).astype(o_ref.dtype)


def _sb_attn_decode(qb, cache_k, cache_v, layer, page_table, bias):
    bsz, hd = qb.shape
    n_pages = page_table.shape[1]
    page = cache_k.shape[2]
    assert page == LANES
    page_spec = pl.BlockSpec((None, None, page, hd),
                             lambda b, p, pt: (layer, pt[b * n_pages + n_pages - 1 - p], 0, 0))
    grid_spec = pltpu.PrefetchScalarGridSpec(
        num_scalar_prefetch=1,
        grid=(bsz, n_pages),
        in_specs=[pl.BlockSpec((SB_HEADS, 1), lambda b, p, pt: (0, 0)),
                  pl.BlockSpec((None, 1, hd), lambda b, p, pt: (b, 0, 0)),
                  page_spec, page_spec],
        out_specs=pl.BlockSpec((None, 1, hd), lambda b, p, pt: (b, 0, 0)),
        scratch_shapes=[pltpu.VMEM((SB_HEADS, hd), F32), pltpu.VMEM((SB_HEADS, LANES), F32)],
    )
    return pl.pallas_call(
        functools.partial(_sb_decode_body, page=page),
        grid_spec=grid_spec,
        out_shape=jax.ShapeDtypeStruct((bsz, 1, hd), F32),
        compiler_params=_cparams("parallel", "arbitrary"),
        name="sb_attn_decode",
    )(page_table.reshape(-1), bias.reshape(SB_HEADS, 1), qb.astype(F32).reshape(bsz, 1, hd), cache_k, cache_v)


def _out2_body(x_ref, a_ref, o_ref, wa_ref, wo_ref, y_ref):
    y_ref[...] = (x_ref[...] + _dot(a_ref[...].astype(BF16), wa_ref[...])
                  + _dot(o_ref[...].astype(BF16), wo_ref[...]))


def _out2(x, a, o, wa, wo):
    t, d = x.shape
    tm = _token_tile(t, 512)
    row = lambda n: pl.BlockSpec((tm, n), lambda i: (i, 0))
    return pl.pallas_call(
        _out2_body,
        grid=(t // tm,),
        in_specs=[row(d), row(a.shape[1]), row(o.shape[1]), _resident(wa.shape), _resident(wo.shape)],
        out_specs=row(d),
        out_shape=jax.ShapeDtypeStruct((t, d), F32),
        compiler_params=_cparams("parallel"),
        name="ab_out",
    )(x, a, o, wa, wo)


def _out1_body(x_ref, y_ref, w_ref, o_ref):
    o_ref[...] = x_ref[...] + _dot(y_ref[...].astype(BF16), w_ref[...])


def _out1(x, y, w):
    t, d = x.shape
    tm = _token_tile(t, 512)
    row = lambda n: pl.BlockSpec((tm, n), lambda i: (i, 0))
    return pl.pallas_call(
        _out1_body,
        grid=(t // tm,),
        in_specs=[row(d), row(y.shape[1]), _resident(w.shape)],
        out_specs=row(d),
        out_shape=jax.ShapeDtypeStruct((t, d), F32),
        compiler_params=_cparams("parallel"),
        name="m_out",
    )(x, y, w)


def _m_in_body(x_ref, nw_ref, wz_ref, wx_ref, wdt_ref, wdtt_ref, z_ref, xbc_ref, dt_ref, dtt_ref):
    h = _rms_bf16(x_ref[...], nw_ref[...])
    z_ref[...] = _dot(h, wz_ref[...])
    xbc_ref[...] = _dot(h, wx_ref[...])
    dt_ref[...] = _dot(h, wdt_ref[...])
    dtt_ref[...] = _dot_nt(wdtt_ref[...], h)


def _m_in(x, nw, wz, wx, wdt, wdtt):
    t, d = x.shape
    tm = _token_tile(t, 512)
    nh = wdt.shape[1]
    row = lambda n: pl.BlockSpec((tm, n), lambda i: (i, 0))
    return pl.pallas_call(
        _m_in_body,
        grid=(t // tm,),
        in_specs=[row(d), _resident((1, d)), _resident(wz.shape), _resident(wx.shape),
                  _resident(wdt.shape), _resident(wdtt.shape)],
        out_specs=[row(wz.shape[1]), row(wx.shape[1]), row(nh), pl.BlockSpec((nh, tm), lambda i: (0, i))],
        out_shape=[jax.ShapeDtypeStruct((t, wz.shape[1]), F32), jax.ShapeDtypeStruct((t, wx.shape[1]), F32),
                   jax.ShapeDtypeStruct((t, nh), F32), jax.ShapeDtypeStruct((nh, t), F32)],
        compiler_params=_cparams("parallel"),
        name="m_in",
    )(x, nw, wz, wx, wdt, wdtt)


_MCONV_LANES = 512


def _m_conv_body(xbc_ref, w_ref, b_ref, x_ref, bm_ref, cm_ref, cn_ref, ext_ref, *, tt, d_inner, gn):
    t = pl.program_id(1)
    halo = SUBLANES
    nprev = M_CONV_W - 1

    @pl.when(t == 0)
    def _():
        ext_ref[0:halo, :] = jnp.zeros((halo, ext_ref.shape[1]), F32)

    ext_ref[halo:halo + tt, :] = xbc_ref[...]
    first = halo - nprev
    cdim = ext_ref.shape[1]
    for lc in range(cdim // _MCONV_LANES):
        ls = slice(lc * _MCONV_LANES, (lc + 1) * _MCONV_LANES)
        for r in range(tt // _CONV_ROWS):
            base = r * _CONV_ROWS + first
            acc = w_ref[0:1, ls] * ext_ref[base:base + _CONV_ROWS, ls]
            for j in range(1, M_CONV_W):
                acc = acc + w_ref[j:j + 1, ls] * ext_ref[base + j:base + j + _CONV_ROWS, ls]
            val = _silu(acc + b_ref[:, ls])
            rows = slice(r * _CONV_ROWS, (r + 1) * _CONV_ROWS)
            lo = lc * _MCONV_LANES
            if lo < d_inner:
                x_ref[rows, lo:lo + _MCONV_LANES] = val
            elif lo < d_inner + gn:
                bm_ref[rows, lo - d_inner:lo - d_inner + _MCONV_LANES] = val
            else:
                cm_ref[rows, lo - d_inner - gn:lo - d_inner - gn + _MCONV_LANES] = val.astype(BF16)

    @pl.when(t == pl.num_programs(1) - 1)
    def _():
        cn_ref[...] = ext_ref[halo + tt - nprev:halo + tt, :]

    ext_ref[0:halo, :] = ext_ref[tt:tt + halo, :]


def _m_conv_prompt(xbc, bsz, w, b, d_inner):
    t_all, cdim = xbc.shape
    gn = M_GROUPS * M_D_STATE
    assert gn == _MCONV_LANES and d_inner % _MCONV_LANES == 0
    t = t_all // bsz
    tt = _token_tile(t, 256)
    nt = t // tt
    row = lambda n: pl.BlockSpec((tt, n), lambda i, j: (i * nt + j, 0))
    return pl.pallas_call(
        functools.partial(_m_conv_body, tt=tt, d_inner=d_inner, gn=gn),
        grid=(bsz, nt),
        in_specs=[row(cdim), _resident(w.shape), _resident((1, cdim))],
        out_specs=[row(d_inner), row(gn), row(gn),
                   pl.BlockSpec((None, M_CONV_W - 1, cdim), lambda i, j: (i, 0, 0))],
        out_shape=[jax.ShapeDtypeStruct((t_all, d_inner), F32), jax.ShapeDtypeStruct((t_all, gn), F32),
                   jax.ShapeDtypeStruct((t_all, gn), BF16),
                   jax.ShapeDtypeStruct((bsz, M_CONV_W - 1, cdim), F32)],
        scratch_shapes=[pltpu.VMEM((SUBLANES + tt, cdim), F32)],
        compiler_params=_cparams("parallel", "arbitrary"),
        name="m_conv_prompt",
    )(xbc, w, b)


def _head_expand_matrix(nh, width):
    h = lax.broadcasted_iota(jnp.int32, (nh, nh * width), 0)
    c = lax.broadcasted_iota(jnp.int32, (nh, nh * width), 1)
    return jnp.where(c // width == h, 1.0, 0.0).astype(BF16)


def _gate_norm(y, z, nw, d_inner):
    y = y * _silu(z)
    gw = d_inner // M_GROUPS
    parts = []
    for g in range(M_GROUPS):
        yg = y[:, g * gw:(g + 1) * gw]
        parts.append(yg * lax.rsqrt(jnp.mean(yg * yg, axis=-1, keepdims=True) + EPS))
    return jnp.concatenate(parts, axis=1) * nw


def _ssd_body(x_ref, bm_ref, cm_ref, z_ref, dt_ref, dtt_ref, dtb_ref, dtbt_ref, a_ref, at_ref, de_ref, nw_ref,
              y_ref, hT_ref, h_ref, yd_ref, *, nh, d_inner):
    c = pl.program_id(1)
    L = SSD_CHUNK
    hpg = nh // M_GROUPS
    gw = hpg * M_HEAD_DIM

    @pl.when(c == 0)
    def _():
        h_ref[...] = jnp.zeros_like(h_ref)

    dt = _softplus(dt_ref[...] + dtb_ref[...])
    dtt = _softplus(dtt_ref[...] + dtbt_ref[...])
    row = lax.broadcasted_iota(jnp.int32, (L, L), 0)
    col = lax.broadcasted_iota(jnp.int32, (L, L), 1)
    causal = col <= row
    tri = jnp.where(causal, 1.0, 0.0).astype(BF16)
    trit = jnp.where(row <= col, 1.0, 0.0).astype(BF16)
    dta_hi, dta_lo = _split2(dt * a_ref[...])
    acs = _dot(tri, dta_hi) + _dot(tri, dta_lo)
    acst = _dot_exact_rhs(_split2(dtt * at_ref[...]), trit)
    acs_last = acs[L - 1:L, :]
    expand = _head_expand_matrix(nh, M_HEAD_DIM)
    dt_e = _dot_exact_rhs(_split3(dt), expand)
    eacs_e = _dot_exact_rhs(_split3(jnp.exp(acs)), expand)
    eend_e = _dot_exact_rhs(_split3(jnp.exp(acs_last - acs)), expand)
    etot_e = _dot_exact_rhs(_split3(jnp.broadcast_to(jnp.exp(acs_last), (SUBLANES, nh))), expand)[0:1, :]

    x = x_ref[...]
    dtx = x * dt_e
    dtx_b = dtx.astype(BF16)
    dtxe_b = (dtx * eend_e).astype(BF16)
    for g in range(M_GROUPS):
        ns = slice(g * M_D_STATE, (g + 1) * M_D_STATE)
        gs = slice(g * gw, (g + 1) * gw)
        bg = bm_ref[:, ns]
        cg = cm_ref[:, ns]
        cb = _dot_nt(cg, bg.astype(BF16))
        hg = h_ref[:, gs]
        yd_ref[:, gs] = _dot(cg, hg.astype(BF16)) * eacs_e[:, gs]
        h_ref[:, gs] = hg * etot_e[:, gs] + _dot(bg.T.astype(BF16), dtxe_b[:, gs])
        for hh in range(hpg):
            head = g * hpg + hh
            hs = slice(head * M_HEAD_DIM, (head + 1) * M_HEAD_DIM)
            dec = jnp.exp(jnp.where(causal, acs[:, head:head + 1] - acst[head:head + 1, :], -jnp.inf))
            yd_ref[:, hs] += _dot((cb * dec).astype(BF16), dtx_b[:, hs])

    y = yd_ref[...] + de_ref[...] * x
    y_ref[...] = _gate_norm(y, z_ref[...], nw_ref[...], d_inner).astype(BF16)

    @pl.when(c == pl.num_programs(1) - 1)
    def _():
        hT_ref[...] = h_ref[...].T


def _ssd_prompt(x, bm, cm, z, dt, dtt, dtb, a, d_e, nw, bsz):
    t_all, d_inner = x.shape
    nh = dt.shape[1]
    gn = bm.shape[1]
    t = t_all // bsz
    L = SSD_CHUNK
    assert t % L == 0
    nc = t // L
    row = lambda n: pl.BlockSpec((L, n), lambda b, c: (b * nc + c, 0))
    return pl.pallas_call(
        functools.partial(_ssd_body, nh=nh, d_inner=d_inner),
        grid=(bsz, nc),
        in_specs=[row(d_inner), row(gn), row(gn), row(d_inner), row(nh),
                  pl.BlockSpec((nh, L), lambda b, c: (0, b * nc + c)),
                  _resident((1, nh)), _resident((nh, 1)), _resident((1, nh)), _resident((nh, 1)),
                  _resident((1, d_inner)), _resident((1, d_inner))],
        out_specs=[row(d_inner), pl.BlockSpec((None, d_inner, M_D_STATE), lambda b, c: (b, 0, 0))],
        out_shape=[jax.ShapeDtypeStruct((t_all, d_inner), BF16),
                   jax.ShapeDtypeStruct((bsz, d_inner, M_D_STATE), F32)],
        scratch_shapes=[pltpu.VMEM((M_D_STATE, d_inner), F32), pltpu.VMEM((L, d_inner), F32)],
        compiler_params=_cparams("parallel", "arbitrary"),
        name="ssd_prompt",
    )(x, bm, cm, z, dt, dtt, dtb.reshape(1, nh), dtb.reshape(nh, 1), a.reshape(1, nh), a.reshape(nh, 1), d_e, nw)


def _m_step_body(cprev_ref, xbc_ref, z_ref, dt_ref, st_ref, w_ref, b_ref, dtb_ref, a_ref, de_ref, nw_ref,
                 y_ref, cn_ref, hn_ref, *, nh, d_inner):
    nprev = M_CONV_W - 1
    gn = M_GROUPS * M_D_STATE
    gw = d_inner // M_GROUPS
    new = xbc_ref[...]
    acc = w_ref[nprev:M_CONV_W, :] * new + b_ref[...]
    for j in range(nprev):
        acc = acc + w_ref[j:j + 1, :] * cprev_ref[j:j + 1, :]
    cn_ref[0:nprev - 1, :] = cprev_ref[1:nprev, :]
    cn_ref[nprev - 1:nprev, :] = new
    xbc = _silu(acc)
    x = xbc[:, :d_inner]

    dt = _softplus(dt_ref[...] + dtb_ref[...])
    expand = _head_expand_matrix(nh, M_HEAD_DIM)
    rows8 = lambda v: jnp.broadcast_to(v, (SUBLANES, nh))
    dt_e = _dot_exact_rhs(_split3(rows8(dt)), expand)[0:1, :]
    da_e = _dot_exact_rhs(_split3(rows8(jnp.exp(dt * a_ref[...]))), expand)[0:1, :]
    dtx = x * dt_e

    rowi = lax.broadcasted_iota(jnp.int32, (M_D_STATE, M_D_STATE), 0)
    b_mat = jnp.zeros((M_D_STATE, M_D_STATE), F32)
    c_mat = jnp.zeros((M_D_STATE, M_D_STATE), F32)
    for g in range(M_GROUPS):
        b_row = xbc[:, d_inner + g * M_D_STATE:d_inner + (g + 1) * M_D_STATE]
        c_row = xbc[:, d_inner + gn + g * M_D_STATE:d_inner + gn + (g + 1) * M_D_STATE]
        b_mat = jnp.where(rowi == g, b_row, b_mat)
        c_mat = jnp.where(rowi == g, c_row, c_mat)
    b_cols = b_mat.T
    c_cols = c_mat.T

    ht = st_ref[...].T
    hn_parts, y_parts = [], []
    for g in range(M_GROUPS):
        gs = slice(g * gw, (g + 1) * gw)
        hn = ht[:, gs] * da_e[:, gs] + b_cols[:, g:g + 1] * dtx[:, gs]
        hn_parts.append(hn)
        y_parts.append(jnp.sum(c_cols[:, g:g + 1] * hn, axis=0, keepdims=True))
    hn_ref[...] = jnp.concatenate(hn_parts, axis=1).T
    y = jnp.concatenate(y_parts, axis=1) + de_ref[...] * x
    y_ref[...] = _gate_norm(y, z_ref[...], nw_ref[...], d_inner)


def _m_step(conv_prev, xbc, z, dt, state, w, b, dtb, a, d_e, nw):
    bsz, nprev, cdim = conv_prev.shape
    d_inner = z.shape[1]
    nh = dt.shape[1]
    per_b = lambda r, n: pl.BlockSpec((None, r, n), lambda i: (i, 0, 0))
    return pl.pallas_call(
        functools.partial(_m_step_body, nh=nh, d_inner=d_inner),
        grid=(bsz,),
        in_specs=[per_b(nprev, cdim), per_b(1, cdim), per_b(1, d_inner), per_b(1, nh), per_b(d_inner, M_D_STATE),
                  _resident(w.shape), _resident((1, cdim)), _resident((1, nh)), _resident((1, nh)),
                  _resident((1, d_inner)), _resident((1, d_inner))],
        out_specs=[per_b(1, d_inner), per_b(nprev, cdim), per_b(d_inner, M_D_STATE)],
        out_shape=[jax.ShapeDtypeStruct((bsz, 1, d_inner), F32),
                   jax.ShapeDtypeStruct((bsz, nprev, cdim), F32),
                   jax.ShapeDtypeStruct((bsz, d_inner, M_D_STATE), F32)],
        compiler_params=_cparams("parallel"),
        name="m_step",
    )(conv_prev, xbc.reshape(bsz, 1, cdim), z.reshape(bsz, 1, d_inner), dt.reshape(bsz, 1, nh),
      state.reshape(bsz, d_inner, M_D_STATE), w, b, dtb.reshape(1, nh), a.reshape(1, nh), d_e, nw)


def kernel(x_prompt, x_sample, cache_sb_k, cache_sb_v, state_conv_a, state_mamba_conv, state_ssm, page_table,
           ffn1_norm, ffn1_gate, ffn1_up, ffn1_down, mix_norm, ffn2_norm, ffn2_gate, ffn2_up, ffn2_down,
           final_norm, ab_in_proj, conv_a_w, conv_a_b, conv_a_ln_g, conv_a_ln_b, sb_beta_bias, ab_out_proj,
           m_in_proj, m_conv_w, m_conv_b, m_dt_bias, m_A_log, m_D, m_norm, m_out_proj):
    bp, seq, d = x_prompt.shape
    bs = x_sample.shape[0]
    depth = ffn1_norm.shape[0]
    hd = SB_HEADS * SB_HEAD_DIM
    conv_ch = conv_a_w.shape[2]
    nh = m_dt_bias.shape[1]
    d_inner = nh * M_HEAD_DIM
    conv_dim = m_conv_w.shape[2]
    n_phys, page = cache_sb_k.shape[1], cache_sb_k.shape[2]

    bf = lambda w: w.astype(BF16)
    f1g, f1u, f1d = bf(ffn1_gate), bf(ffn1_up), bf(ffn1_down)
    f2g, f2u, f2d = bf(ffn2_gate), bf(ffn2_up), bf(ffn2_down)
    ab_in_w, ab_out_w = bf(ab_in_proj), bf(ab_out_proj)
    m_in_w, m_out_w = bf(m_in_proj), bf(m_out_proj)
    fw = final_norm.reshape(1, d)
    cache_k = cache_sb_k.reshape(cache_sb_k.shape[0], n_phys, page, hd)
    cache_v = cache_sb_v.reshape(cache_sb_v.shape[0], n_phys, page, hd)

    xp = x_prompt.reshape(bp * seq, d)
    xs = x_sample.reshape(bs, d)
    outs = {k: [] for k in ("kp", "vp", "cap", "mcp", "ssp", "ks", "vs", "cas", "mcs", "sss")}
    n1 = ffn1_norm.reshape(depth, 1, d)
    n2 = ffn2_norm.reshape(depth, 1, d)
    for layer in range(depth):
        w1 = (n1, f1g, f1u, f1d, fw, layer)
        xp = _ffn(xp, *w1)
        xs = _ffn(xs, *w1)
        mnw = mix_norm[layer].reshape(1, d)
        if layer % 2 == 0:
            e = layer // 2
            cw = (conv_a_w[e], conv_a_b[e].reshape(1, conv_ch), conv_a_ln_g[e].reshape(1, conv_ch),
                  conv_a_ln_b[e].reshape(1, conv_ch))
            wa, wo = ab_out_w[e][:conv_ch], ab_out_w[e][conv_ch:]
            u, qb, k, v, kb, vb = _ab_in(xp, mnw, ab_in_w[e])
            a_out, cvp = _conv_a_prompt(u, bp, *cw)
            o = _sb_attn_prompt(qb, kb, vb, sb_beta_bias[e], bp)
            xp = _out2(xp, a_out, o, wa, wo)
            outs["kp"].append(k.reshape(bp, seq, SB_HEADS, SB_HEAD_DIM))
            outs["vp"].append(v.reshape(bp, seq, SB_HEADS, SB_HEAD_DIM))
            outs["cap"].append(cvp)

            u, qb, k, v, _, _ = _ab_in(xs, mnw, ab_in_w[e])
            a_out, cvs = _conv_a_sample(state_conv_a[e], u, *cw)
            o = _sb_attn_decode(qb, cache_k, cache_v, e, page_table, sb_beta_bias[e])
            xs = _out2(xs, a_out.reshape(bs, conv_ch), o.reshape(bs, hd), wa, wo)
            outs["ks"].append(k.reshape(bs, 1, SB_HEADS, SB_HEAD_DIM))
            outs["vs"].append(v.reshape(bs, 1, SB_HEADS, SB_HEAD_DIM))
            outs["cas"].append(cvs)
        else:
            o = layer // 2
            w_in = m_in_w[o]
            wz, wx, wdt = w_in[:, :d_inner], w_in[:, d_inner:d_inner + conv_dim], w_in[:, d_inner + conv_dim:]
            wdtt = wdt.T
            a = -jnp.exp(m_A_log[o])
            d_e = jnp.repeat(m_D[o], M_HEAD_DIM).reshape(1, d_inner)
            nw = m_norm[o].reshape(1, d_inner)
            cb = m_conv_b[o].reshape(1, conv_dim)

            z, xbc, dt, dtt = _m_in(xp, mnw, wz, wx, wdt, wdtt)
            x, bm, cm, mcp = _m_conv_prompt(xbc, bp, m_conv_w[o], cb, d_inner)
            y, hT = _ssd_prompt(x, bm, cm, z, dt, dtt, m_dt_bias[o], a, d_e, nw, bp)
            xp = _out1(xp, y, m_out_w[o])
            outs["mcp"].append(mcp)
            outs["ssp"].append(hT.reshape(bp, nh, M_HEAD_DIM, M_D_STATE))

            z, xbc, dt, _ = _m_in(xs, mnw, wz, wx, wdt, wdtt)
            y, mcs, hn = _m_step(state_mamba_conv[o], xbc, z, dt, state_ssm[o], m_conv_w[o], cb,
                                 m_dt_bias[o], a, d_e, nw)
            xs = _out1(xs, y.reshape(bs, d_inner), m_out_w[o])
            outs["mcs"].append(mcs)
            outs["sss"].append(hn.reshape(bs, nh, M_HEAD_DIM, M_D_STATE))
        last = layer == depth - 1
        w2 = (n2, f2g, f2u, f2d, fw, layer)
        xp = _ffn(xp, *w2, final=last)
        xs = _ffn(xs, *w2, final=last)
    st = lambda key: jnp.stack(outs[key])
    return (xp.reshape(bp, seq, d), xs.reshape(bs, 1, d),
            st("kp"), st("vp"), st("cap"), st("mcp"), st("ssp"),
            st("ks"), st("vs"), st("cas"), st("mcs"), st("sss"))
```

```python
import functools

import jax
import jax.numpy as jnp
from jax import lax
from jax.experimental import pallas as pl
from jax.experimental.pallas import tpu as pltpu

F32 = jnp.float32
BF16 = jnp.bfloat16
EPS = 1e-6

LANES = 128
SUBLANES = 8
VMEM_LIMIT_BYTES = 56 * 1024 * 1024

SB_HEADS = 8
SB_HEAD_DIM = 64
CONV_W = 31
M_HEAD_DIM = 64
M_GROUPS = 4
M_D_STATE = 128
M_CONV_W = 4
SSD_CHUNK = 128


def _cparams(*sem):
    return pltpu.CompilerParams(dimension_semantics=sem, vmem_limit_bytes=VMEM_LIMIT_BYTES)


def _resident(shape):
    nd = len(shape)
    return pl.BlockSpec(shape, lambda *_: (0,) * nd, pipeline_mode=pl.Buffered(1))


def _resident_layer(shape, layer):
    nd = len(shape)
    return pl.BlockSpec((None,) + tuple(shape[1:]), lambda *_: (layer,) + (0,) * (nd - 1),
                        pipeline_mode=pl.Buffered(1))


def _rms_bf16(x, w):
    y = x * lax.rsqrt(jnp.mean(x * x, axis=-1, keepdims=True) + EPS)
    return (y * w).astype(BF16)


def _dot(a, b):
    return jnp.dot(a, b, preferred_element_type=F32)


def _dot_nt(a, b):
    return lax.dot_general(a, b, (((1,), (1,)), ((), ())), preferred_element_type=F32)


def _split2(v):
    hi = v.astype(BF16)
    lo = (v - hi.astype(F32)).astype(BF16)
    return hi, lo


def _split3(v):
    hi = v.astype(BF16)
    r = v - hi.astype(F32)
    mid = r.astype(BF16)
    lo = (r - mid.astype(F32)).astype(BF16)
    return hi, mid, lo


def _dot_exact_rhs(parts, m):
    acc = _dot(parts[0], m)
    for p in parts[1:]:
        acc = acc + _dot(p, m)
    return acc


def _softplus(z):
    return jnp.maximum(z, 0.0) + jnp.log1p(jnp.exp(-jnp.abs(z)))


def _silu(v):
    return v * jax.nn.sigmoid(v)


def _token_tile(t, want):
    return want if t % want == 0 else t


def _ffn_body(x_ref, nw_ref, wg_ref, wu_ref, wd_ref, fw_ref, o_ref, *, final):
    x = x_ref[...]
    h = _rms_bf16(x, nw_ref[...])
    g = _dot(h, wg_ref[...])
    u = _dot(h, wu_ref[...])
    a = (_silu(g) * u).astype(BF16)
    y = x + 0.5 * _dot(a, wd_ref[...])
    if final:
        y = y * lax.rsqrt(jnp.mean(y * y, axis=-1, keepdims=True) + EPS) * fw_ref[...]
    o_ref[...] = y


def _ffn(x, nw, wg, wu, wd, fw, layer, *, final=False):
    t, d = x.shape
    tm = _token_tile(t, 512)
    return pl.pallas_call(
        functools.partial(_ffn_body, final=final),
        grid=(t // tm,),
        in_specs=[pl.BlockSpec((tm, d), lambda i: (i, 0)),
                  _resident_layer(nw.shape, layer), _resident_layer(wg.shape, layer),
                  _resident_layer(wu.shape, layer), _resident_layer(wd.shape, layer),
                  _resident((1, d))],
        out_specs=pl.BlockSpec((tm, d), lambda i: (i, 0)),
        out_shape=jax.ShapeDtypeStruct((t, d), F32),
        compiler_params=_cparams("parallel"),
        name="ffn",
    )(x, nw, wg, wu, wd, fw)


def _ab_in_body(x_ref, nw_ref, w_ref, u_ref, qb_ref, k_ref, v_ref, kb_ref, vb_ref, *, c, hd):
    h = _rms_bf16(x_ref[...], nw_ref[...])
    p = _dot(h, w_ref[...])
    u_ref[...] = p[:, :c] * jax.nn.sigmoid(p[:, c:2 * c])
    qb_ref[...] = (p[:, 2 * c:2 * c + hd] * (SB_HEAD_DIM ** -0.5)).astype(BF16)
    k = p[:, 2 * c + hd:2 * c + 2 * hd]
    v = p[:, 2 * c + 2 * hd:]
    k_ref[...] = k
    v_ref[...] = v
    kb_ref[...] = k.astype(BF16)
    vb_ref[...] = v.astype(BF16)


def _ab_in(x, nw, w):
    t, d = x.shape
    hd = SB_HEADS * SB_HEAD_DIM
    c = (w.shape[1] - 3 * hd) // 2
    tm = _token_tile(t, 512)
    row = lambda n: pl.BlockSpec((tm, n), lambda i: (i, 0))
    return pl.pallas_call(
        functools.partial(_ab_in_body, c=c, hd=hd),
        grid=(t // tm,),
        in_specs=[row(d), _resident((1, d)), _resident(w.shape)],
        out_specs=[row(c), row(hd), row(hd), row(hd), row(hd), row(hd)],
        out_shape=[jax.ShapeDtypeStruct((t, c), F32), jax.ShapeDtypeStruct((t, hd), BF16),
                   jax.ShapeDtypeStruct((t, hd), F32), jax.ShapeDtypeStruct((t, hd), F32),
                   jax.ShapeDtypeStruct((t, hd), BF16), jax.ShapeDtypeStruct((t, hd), BF16)],
        compiler_params=_cparams("parallel"),
        name="ab_in",
    )(x, nw, w)


_CONV_ROWS = 32
_CONV_HALO = 32


def _layernorm_silu(cv, g, b):
    mu = jnp.mean(cv, axis=-1, keepdims=True)
    d = cv - mu
    var = jnp.mean(d * d, axis=-1, keepdims=True)
    return _silu(d * lax.rsqrt(var + EPS) * g + b)


def _conv_a_body(u_ref, w_ref, b_ref, g_ref, be_ref, a_ref, cn_ref, ext_ref, *, tt):
    t = pl.program_id(1)

    @pl.when(t == 0)
    def _():
        ext_ref[0:_CONV_HALO, :] = jnp.zeros((_CONV_HALO, ext_ref.shape[1]), F32)

    ext_ref[_CONV_HALO:_CONV_HALO + tt, :] = u_ref[...]
    first = _CONV_HALO - (CONV_W - 1)
    for r in range(tt // _CONV_ROWS):
        base = r * _CONV_ROWS + first
        acc = w_ref[0:1, :] * ext_ref[base:base + _CONV_ROWS, :]
        for j in range(1, CONV_W):
            acc = acc + w_ref[j:j + 1, :] * ext_ref[base + j:base + j + _CONV_ROWS, :]
        cv = acc + b_ref[...]
        a_ref[r * _CONV_ROWS:(r + 1) * _CONV_ROWS, :] = _layernorm_silu(cv, g_ref[...], be_ref[...]).astype(BF16)

    @pl.when(t == pl.num_programs(1) - 1)
    def _():
        cn_ref[...] = ext_ref[_CONV_HALO + tt - (CONV_W - 1):_CONV_HALO + tt, :]

    ext_ref[0:_CONV_HALO, :] = ext_ref[tt:tt + _CONV_HALO, :]


def _conv_a_prompt(u, bsz, w, b, g, be):
    t_all, c = u.shape
    t = t_all // bsz
    tt = _token_tile(t, 512)
    nt = t // tt
    return pl.pallas_call(
        functools.partial(_conv_a_body, tt=tt),
        grid=(bsz, nt),
        in_specs=[pl.BlockSpec((tt, c), lambda i, j: (i * nt + j, 0)),
                  _resident(w.shape), _resident((1, c)), _resident((1, c)), _resident((1, c))],
        out_specs=[pl.BlockSpec((tt, c), lambda i, j: (i * nt + j, 0)),
                   pl.BlockSpec((None, CONV_W - 1, c), lambda i, j: (i, 0, 0))],
        out_shape=[jax.ShapeDtypeStruct((t_all, c), BF16),
                   jax.ShapeDtypeStruct((bsz, CONV_W - 1, c), F32)],
        scratch_shapes=[pltpu.VMEM((_CONV_HALO + tt, c), F32)],
        compiler_params=_cparams("parallel", "arbitrary"),
        name="conv_a_prompt",
    )(u, w, b, g, be)


def _conv_a_sample_body(prev_ref, u_ref, w_ref, b_ref, g_ref, be_ref, a_ref, cn_ref):
    nprev = CONV_W - 1
    prev = prev_ref[...]
    u = u_ref[...]
    cv = jnp.sum(w_ref[0:nprev, :] * prev, axis=0, keepdims=True) + w_ref[nprev:CONV_W, :] * u + b_ref[...]
    a_ref[...] = _layernorm_silu(cv, g_ref[...], be_ref[...])
    cn_ref[0:nprev - 1, :] = prev_ref[1:nprev, :]
    cn_ref[nprev - 1:nprev, :] = u


def _conv_a_sample(prev, u, w, b, g, be):
    bsz, nprev, c = prev.shape
    return pl.pallas_call(
        _conv_a_sample_body,
        grid=(bsz,),
        in_specs=[pl.BlockSpec((None, nprev, c), lambda i: (i, 0, 0)),
                  pl.BlockSpec((None, 1, c), lambda i: (i, 0, 0)),
                  _resident(w.shape), _resident((1, c)), _resident((1, c)), _resident((1, c))],
        out_specs=[pl.BlockSpec((None, 1, c), lambda i: (i, 0, 0)),
                   pl.BlockSpec((None, nprev, c), lambda i: (i, 0, 0))],
        out_shape=[jax.ShapeDtypeStruct((bsz, 1, c), F32),
                   jax.ShapeDtypeStruct((bsz, nprev, c), F32)],
        compiler_params=_cparams("parallel"),
        name="conv_a_sample",
    )(prev, u.reshape(bsz, 1, c), w, b, g, be)


LOG2E = 1.4426950408889634


def _suffix_matrix(tk, with_total):
    cols = tk + LANES if with_total else tk
    j = lax.broadcasted_iota(jnp.int32, (tk, cols), 0)
    s = lax.broadcasted_iota(jnp.int32, (tk, cols), 1)
    return jnp.where((j > s) | (s >= tk), 1.0, 0.0).astype(BF16)


def _neg_abs(x):
    return lax.bitcast_convert_type(lax.bitcast_convert_type(x, jnp.uint32) | jnp.uint32(0x80000000), F32)


_SB_STRIP = 32


def _sb_block(z, bias2, v_masked, sufm, carry, diagonal):
    tq, tk = z.shape
    strips = [slice(r * _SB_STRIP, (r + 1) * _SB_STRIP) for r in range(tq // _SB_STRIP)]

    def admitted(r):
        rowi = lax.broadcasted_iota(jnp.int32, (_SB_STRIP, tk), 0) + r * _SB_STRIP
        coli = lax.broadcasted_iota(jnp.int32, (_SB_STRIP, tk), 1)
        return coli < rowi

    sp_parts, arg_parts, first = [], [], []
    for r, rows in enumerate(strips):
        zs = z[rows] * LOG2E + bias2
        sp = jnp.maximum(zs, 0.0) + jnp.log2(1.0 + jnp.exp2(_neg_abs(zs)))
        arg_parts.append(zs - sp)
        if diagonal:
            sp = jnp.where(admitted(r), sp, 0.0)
        first.append(sp[:, 0:1])
        sp_parts.append(sp.astype(BF16))
    sums = _dot(jnp.concatenate(sp_parts, axis=0), sufm)
    w_parts = []
    for r, rows in enumerate(strips):
        cr = carry[rows]
        after = jnp.concatenate([sums[rows, c * LANES:(c + 1) * LANES] + cr for c in range(tk // LANES)], axis=1)
        wgt = jnp.exp2(arg_parts[r] - after)
        if diagonal:
            wgt = jnp.where(admitted(r), wgt, 0.0)
        w_parts.append(wgt.astype(BF16))
    out = _dot(jnp.concatenate(w_parts, axis=0), v_masked)
    total = sums[:, 0:1] + jnp.concatenate(first, axis=0)
    return out, carry + jnp.broadcast_to(total, carry.shape)


def _sb_prompt_body(bias_ref, q_ref, k_ref, v_ref, o_ref, qm_ref, acc_ref, carry_ref, *, tq):
    i = pl.program_id(1)
    tk = tq
    sufm = _suffix_matrix(tk, False)
    lane = lax.broadcasted_iota(jnp.int32, (1, LANES), 1)
    sels = [lane < SB_HEAD_DIM, lane >= SB_HEAD_DIM]
    for head in range(SB_HEADS):
        qp = q_ref[:, (head // 2) * LANES:(head // 2 + 1) * LANES]
        qm_ref[head] = jnp.where(sels[head % 2], qp, jnp.zeros_like(qp))
    acc_ref[...] = jnp.zeros_like(acc_ref)
    carry_ref[...] = jnp.zeros_like(carry_ref)

    def block(kj, diagonal):
        start = pl.multiple_of(kj * tk, tk)
        for hp in range(SB_HEADS // 2):
            lanes = slice(hp * LANES, (hp + 1) * LANES)
            kb = k_ref[pl.ds(start, tk), lanes]
            vb = v_ref[pl.ds(start, tk), lanes]
            out = None
            for hh in range(2):
                head = 2 * hp + hh
                vm = jnp.where(sels[hh], vb, jnp.zeros_like(vb))
                o, c2 = _sb_block(_dot_nt(qm_ref[head], kb), bias_ref[head] * LOG2E, vm, sufm,
                                  carry_ref[head], diagonal)
                carry_ref[head] = c2
                out = o if out is None else out + o
            acc_ref[:, lanes] += out

    block(i, True)

    def step(jj, _):
        block(i - jj, False)
        return 0

    lax.fori_loop(1, i + 1, step, 0)
    o_ref[...] = acc_ref[...].astype(o_ref.dtype)


def _sb_attn_prompt(qb, kb, vb, bias, bsz):
    t_all, hd = qb.shape
    t = t_all // bsz
    tq = _token_tile(t, 256)
    nq = t // tq
    return pl.pallas_call(
        functools.partial(_sb_prompt_body, tq=tq),
        grid=(bsz, nq),
        in_specs=[pl.BlockSpec(memory_space=pltpu.SMEM),
                  pl.BlockSpec((tq, hd), lambda b, i: (b * nq + i, 0)),
                  pl.BlockSpec((t, hd), lambda b, i: (b, 0)),
                  pl.BlockSpec((t, hd), lambda b, i: (b, 0))],
        out_specs=pl.BlockSpec((tq, hd), lambda b, i: (b * nq + i, 0)),
        out_shape=jax.ShapeDtypeStruct((t_all, hd), BF16),
        scratch_shapes=[pltpu.VMEM((SB_HEADS, tq, LANES), BF16), pltpu.VMEM((tq, hd), F32),
                        pltpu.VMEM((SB_HEADS, tq, LANES), F32)],
        compiler_params=_cparams("parallel", "arbitrary"),
        name="sb_attn_prompt",
    )(bias, qb, kb, vb)


_DEC_PAGES = 8
_DEC_CHUNK = 256


def _sb_decode_body(pt_ref, bias_ref, q_ref, *refs, page, npg):
    k_refs, v_refs = refs[:npg], refs[npg:2 * npg]
    o_ref, acc_ref, carry_ref = refs[2 * npg:]
    p = pl.program_id(1)
    width = page * SB_HEADS
    nchunk = width // _DEC_CHUNK

    @pl.when(p == 0)
    def _():
        acc_ref[...] = jnp.zeros_like(acc_ref)
        carry_ref[...] = jnp.zeros_like(carry_ref)

    q8 = q_ref[...].astype(BF16)
    own = (lax.broadcasted_iota(jnp.int32, (SB_HEADS, width), 1) % SB_HEADS
           == lax.broadcasted_iota(jnp.int32, (SB_HEADS, width), 0))
    sufm = _suffix_matrix(_DEC_CHUNK, True)
    args, pieces = [], []
    for g in range(npg):
        k2 = k_refs[g][...].reshape(width, SB_HEAD_DIM).astype(BF16)
        z = _dot_nt(q8, k2) + bias_ref[...]
        sp = _softplus(z)
        args.append(z - sp)
        spm = jnp.where(own, sp, 0.0)
        pieces += [spm[:, c * _DEC_CHUNK:(c + 1) * _DEC_CHUNK] for c in range(nchunk)]
    sums = _dot_exact_rhs(_split2(jnp.concatenate(pieces, axis=0)), sufm)
    acc = acc_ref[...]
    run = carry_ref[...]
    for g in range(npg):
        after = [None] * nchunk
        for c in reversed(range(nchunk)):
            rows = slice((g * nchunk + c) * SB_HEADS, (g * nchunk + c + 1) * SB_HEADS)
            after[c] = sums[rows, :_DEC_CHUNK] + jnp.concatenate([run] * (_DEC_CHUNK // LANES), axis=1)
            run = run + sums[rows, _DEC_CHUNK:]
        wgt = jnp.where(own, jnp.exp(args[g] - jnp.concatenate(after, axis=1)), 0.0)
        v2 = v_refs[g][...].reshape(width, SB_HEAD_DIM).astype(BF16)
        acc = acc + _dot(wgt.astype(BF16), v2)
    acc_ref[...] = acc
    carry_ref[...] = run

    @pl.when(p == pl.num_programs(1) - 1)
    def _():
        o_ref[...] = acc


def _sb_attn_decode(q, cache_k, cache_v, layer, page_table, bias):
    bsz = q.shape[0]
    n_pages = page_table.shape[1]
    page = cache_k.shape[2]
    npg = _DEC_PAGES if n_pages % _DEC_PAGES == 0 else 1
    assert (page * SB_HEADS) % _DEC_CHUNK == 0

    def page_spec(g):
        return pl.BlockSpec((None, None, page, SB_HEADS, SB_HEAD_DIM),
                            lambda b, p, pt: (layer, pt[b * n_pages + n_pages - 1 - (p * npg + g)], 0, 0, 0))

    per_b = pl.BlockSpec((None, SB_HEADS, SB_HEAD_DIM), lambda b, p, pt: (b, 0, 0))
    grid_spec = pltpu.PrefetchScalarGridSpec(
        num_scalar_prefetch=1,
        grid=(bsz, n_pages // npg),
        in_specs=[pl.BlockSpec((SB_HEADS, 1), lambda b, p, pt: (0, 0)), per_b]
                 + [page_spec(g) for g in range(npg)] * 2,
        out_specs=per_b,
        scratch_shapes=[pltpu.VMEM((SB_HEADS, SB_HEAD_DIM), F32), pltpu.VMEM((SB_HEADS, LANES), F32)],
    )
    return pl.pallas_call(
        functools.partial(_sb_decode_body, page=page, npg=npg),
        grid_spec=grid_spec,
        out_shape=jax.ShapeDtypeStruct((bsz, SB_HEADS, SB_HEAD_DIM), F32),
        compiler_params=_cparams("parallel", "arbitrary"),
        name="sb_attn_decode",
    )(page_table.reshape(-1), bias.reshape(SB_HEADS, 1), q, *([cache_k] * npg), *([cache_v] * npg))


def _out2_body(x_ref, a_ref, o_ref, wa_ref, wo_ref, y_ref):
    y_ref[...] = (x_ref[...] + _dot(a_ref[...].astype(BF16), wa_ref[...])
                  + _dot(o_ref[...].astype(BF16), wo_ref[...]))


def _out2(x, a, o, wa, wo):
    t, d = x.shape
    tm = _token_tile(t, 512)
    row = lambda n: pl.BlockSpec((tm, n), lambda i: (i, 0))
    return pl.pallas_call(
        _out2_body,
        grid=(t // tm,),
        in_specs=[row(d), row(a.shape[1]), row(o.shape[1]), _resident(wa.shape), _resident(wo.shape)],
        out_specs=row(d),
        out_shape=jax.ShapeDtypeStruct((t, d), F32),
        compiler_params=_cparams("parallel"),
        name="ab_out",
    )(x, a, o, wa, wo)


def _out1_body(x_ref, y_ref, w_ref, o_ref):
    o_ref[...] = x_ref[...] + _dot(y_ref[...].astype(BF16), w_ref[...])


def _out1(x, y, w):
    t, d = x.shape
    tm = _token_tile(t, 512)
    row = lambda n: pl.BlockSpec((tm, n), lambda i: (i, 0))
    return pl.pallas_call(
        _out1_body,
        grid=(t // tm,),
        in_specs=[row(d), row(y.shape[1]), _resident(w.shape)],
        out_specs=row(d),
        out_shape=jax.ShapeDtypeStruct((t, d), F32),
        compiler_params=_cparams("parallel"),
        name="m_out",
    )(x, y, w)


def _m_in_body(x_ref, nw_ref, wz_ref, wx_ref, wdt_ref, wdtt_ref, z_ref, xbc_ref, dt_ref, dtt_ref):
    h = _rms_bf16(x_ref[...], nw_ref[...])
    z_ref[...] = _dot(h, wz_ref[...])
    xbc_ref[...] = _dot(h, wx_ref[...])
    dt_ref[...] = _dot(h, wdt_ref[...])
    dtt_ref[...] = _dot_nt(wdtt_ref[...], h)


def _m_in(x, nw, wz, wx, wdt, wdtt):
    t, d = x.shape
    tm = _token_tile(t, 512)
    nh = wdt.shape[1]
    row = lambda n: pl.BlockSpec((tm, n), lambda i: (i, 0))
    return pl.pallas_call(
        _m_in_body,
        grid=(t // tm,),
        in_specs=[row(d), _resident((1, d)), _resident(wz.shape), _resident(wx.shape),
                  _resident(wdt.shape), _resident(wdtt.shape)],
        out_specs=[row(wz.shape[1]), row(wx.shape[1]), row(nh), pl.BlockSpec((nh, tm), lambda i: (0, i))],
        out_shape=[jax.ShapeDtypeStruct((t, wz.shape[1]), F32), jax.ShapeDtypeStruct((t, wx.shape[1]), F32),
                   jax.ShapeDtypeStruct((t, nh), F32), jax.ShapeDtypeStruct((nh, t), F32)],
        compiler_params=_cparams("parallel"),
        name="m_in",
    )(x, nw, wz, wx, wdt, wdtt)


_MCONV_LANES = 512


def _m_conv_body(xbc_ref, w_ref, b_ref, x_ref, bm_ref, cm_ref, cn_ref, ext_ref, *, tt, d_inner, gn):
    t = pl.program_id(1)
    halo = SUBLANES
    nprev = M_CONV_W - 1

    @pl.when(t == 0)
    def _():
        ext_ref[0:halo, :] = jnp.zeros((halo, ext_ref.shape[1]), F32)

    ext_ref[halo:halo + tt, :] = xbc_ref[...]
    first = halo - nprev
    cdim = ext_ref.shape[1]
    for lc in range(cdim // _MCONV_LANES):
        ls = slice(lc * _MCONV_LANES, (lc + 1) * _MCONV_LANES)
        for r in range(tt // _CONV_ROWS):
            base = r * _CONV_ROWS + first
            acc = w_ref[0:1, ls] * ext_ref[base:base + _CONV_ROWS, ls]
            for j in range(1, M_CONV_W):
                acc = acc + w_ref[j:j + 1, ls] * ext_ref[base + j:base + j + _CONV_ROWS, ls]
            val = _silu(acc + b_ref[:, ls])
            rows = slice(r * _CONV_ROWS, (r + 1) * _CONV_ROWS)
            lo = lc * _MCONV_LANES
            if lo < d_inner:
                x_ref[rows, lo:lo + _MCONV_LANES] = val
            elif lo < d_inner + gn:
                bm_ref[rows, lo - d_inner:lo - d_inner + _MCONV_LANES] = val
            else:
                cm_ref[rows, lo - d_inner - gn:lo - d_inner - gn + _MCONV_LANES] = val.astype(BF16)

    @pl.when(t == pl.num_programs(1) - 1)
    def _():
        cn_ref[...] = ext_ref[halo + tt - nprev:halo + tt, :]

    ext_ref[0:halo, :] = ext_ref[tt:tt + halo, :]


def _m_conv_prompt(xbc, bsz, w, b, d_inner):
    t_all, cdim = xbc.shape
    gn = M_GROUPS * M_D_STATE
    assert gn == _MCONV_LANES and d_inner % _MCONV_LANES == 0
    t = t_all // bsz
    tt = _token_tile(t, 256)
    nt = t // tt
    row = lambda n: pl.BlockSpec((tt, n), lambda i, j: (i * nt + j, 0))
    return pl.pallas_call(
        functools.partial(_m_conv_body, tt=tt, d_inner=d_inner, gn=gn),
        grid=(bsz, nt),
        in_specs=[row(cdim), _resident(w.shape), _resident((1, cdim))],
        out_specs=[row(d_inner), row(gn), row(gn),
                   pl.BlockSpec((None, M_CONV_W - 1, cdim), lambda i, j: (i, 0, 0))],
        out_shape=[jax.ShapeDtypeStruct((t_all, d_inner), F32), jax.ShapeDtypeStruct((t_all, gn), F32),
                   jax.ShapeDtypeStruct((t_all, gn), BF16),
                   jax.ShapeDtypeStruct((bsz, M_CONV_W - 1, cdim), F32)],
        scratch_shapes=[pltpu.VMEM((SUBLANES + tt, cdim), F32)],
        compiler_params=_cparams("parallel", "arbitrary"),
        name="m_conv_prompt",
    )(xbc, w, b)


def _head_expand_matrix(nh, width):
    h = lax.broadcasted_iota(jnp.int32, (nh, nh * width), 0)
    c = lax.broadcasted_iota(jnp.int32, (nh, nh * width), 1)
    return jnp.where(c // width == h, 1.0, 0.0).astype(BF16)


def _gate_norm(y, z, nw, d_inner):
    y = y * _silu(z)
    gw = d_inner // M_GROUPS
    parts = []
    for g in range(M_GROUPS):
        yg = y[:, g * gw:(g + 1) * gw]
        parts.append(yg * lax.rsqrt(jnp.mean(yg * yg, axis=-1, keepdims=True) + EPS))
    return jnp.concatenate(parts, axis=1) * nw


def _ssd_body(x_ref, bm_ref, cm_ref, z_ref, dt_ref, dtt_ref, dtb_ref, dtbt_ref, a_ref, at_ref, de_ref, nw_ref,
              y_ref, hT_ref, h_ref, yd_ref, *, nh, d_inner):
    c = pl.program_id(1)
    L = SSD_CHUNK
    hpg = nh // M_GROUPS
    gw = hpg * M_HEAD_DIM

    @pl.when(c == 0)
    def _():
        h_ref[...] = jnp.zeros_like(h_ref)

    dt = _softplus(dt_ref[...] + dtb_ref[...])
    dtt = _softplus(dtt_ref[...] + dtbt_ref[...])
    row = lax.broadcasted_iota(jnp.int32, (L, L), 0)
    col = lax.broadcasted_iota(jnp.int32, (L, L), 1)
    causal = col <= row
    tri = jnp.where(causal, 1.0, 0.0).astype(BF16)
    trit = jnp.where(row <= col, 1.0, 0.0).astype(BF16)
    dta_hi, dta_lo = _split2(dt * a_ref[...])
    acs = _dot(tri, dta_hi) + _dot(tri, dta_lo)
    acst = _dot_exact_rhs(_split2(dtt * at_ref[...]), trit)
    acs_last = acs[L - 1:L, :]
    expand = _head_expand_matrix(nh, M_HEAD_DIM)
    dt_e = _dot_exact_rhs(_split3(dt), expand)
    eacs_e = _dot_exact_rhs(_split3(jnp.exp(acs)), expand)
    eend_e = _dot_exact_rhs(_split3(jnp.exp(acs_last - acs)), expand)
    etot_e = _dot_exact_rhs(_split3(jnp.broadcast_to(jnp.exp(acs_last), (SUBLANES, nh))), expand)[0:1, :]

    x = x_ref[...]
    dtx = x * dt_e
    dtx_b = dtx.astype(BF16)
    dtxe_b = (dtx * eend_e).astype(BF16)
    for g in range(M_GROUPS):
        ns = slice(g * M_D_STATE, (g + 1) * M_D_STATE)
        gs = slice(g * gw, (g + 1) * gw)
        bg = bm_ref[:, ns]
        cg = cm_ref[:, ns]
        cb = _dot_nt(cg, bg.astype(BF16))
        hg = h_ref[:, gs]
        yd_ref[:, gs] = _dot(cg, hg.astype(BF16)) * eacs_e[:, gs]
        h_ref[:, gs] = hg * etot_e[:, gs] + _dot(bg.T.astype(BF16), dtxe_b[:, gs])
        for hh in range(hpg):
            head = g * hpg + hh
            hs = slice(head * M_HEAD_DIM, (head + 1) * M_HEAD_DIM)
            dec = jnp.exp(jnp.where(causal, acs[:, head:head + 1] - acst[head:head + 1, :], -jnp.inf))
            yd_ref[:, hs] += _dot((cb * dec).astype(BF16), dtx_b[:, hs])

    y = yd_ref[...] + de_ref[...] * x
    y_ref[...] = _gate_norm(y, z_ref[...], nw_ref[...], d_inner).astype(BF16)

    @pl.when(c == pl.num_programs(1) - 1)
    def _():
        hT_ref[...] = h_ref[...].T


def _ssd_prompt(x, bm, cm, z, dt, dtt, dtb, a, d_e, nw, bsz):
    t_all, d_inner = x.shape
    nh = dt.shape[1]
    gn = bm.shape[1]
    t = t_all // bsz
    L = SSD_CHUNK
    assert t % L == 0
    nc = t // L
    row = lambda n: pl.BlockSpec((L, n), lambda b, c: (b * nc + c, 0))
    return pl.pallas_call(
        functools.partial(_ssd_body, nh=nh, d_inner=d_inner),
        grid=(bsz, nc),
        in_specs=[row(d_inner), row(gn), row(gn), row(d_inner), row(nh),
                  pl.BlockSpec((nh, L), lambda b, c: (0, b * nc + c)),
                  _resident((1, nh)), _resident((nh, 1)), _resident((1, nh)), _resident((nh, 1)),
                  _resident((1, d_inner)), _resident((1, d_inner))],
        out_specs=[row(d_inner), pl.BlockSpec((None, d_inner, M_D_STATE), lambda b, c: (b, 0, 0))],
        out_shape=[jax.ShapeDtypeStruct((t_all, d_inner), BF16),
                   jax.ShapeDtypeStruct((bsz, d_inner, M_D_STATE), F32)],
        scratch_shapes=[pltpu.VMEM((M_D_STATE, d_inner), F32), pltpu.VMEM((L, d_inner), F32)],
        compiler_params=_cparams("parallel", "arbitrary"),
        name="ssd_prompt",
    )(x, bm, cm, z, dt, dtt, dtb.reshape(1, nh), dtb.reshape(nh, 1), a.reshape(1, nh), a.reshape(nh, 1), d_e, nw)


def _m_step_body(cprev_ref, xbc_ref, z_ref, dt_ref, st_ref, w_ref, b_ref, dtb_ref, a_ref, de_ref, nw_ref,
                 y_ref, cn_ref, hn_ref, *, nh, d_inner):
    nprev = M_CONV_W - 1
    gn = M_GROUPS * M_D_STATE
    gw = d_inner // M_GROUPS
    new = xbc_ref[...]
    acc = w_ref[nprev:M_CONV_W, :] * new + b_ref[...]
    for j in range(nprev):
        acc = acc + w_ref[j:j + 1, :] * cprev_ref[j:j + 1, :]
    cn_ref[0:nprev - 1, :] = cprev_ref[1:nprev, :]
    cn_ref[nprev - 1:nprev, :] = new
    xbc = _silu(acc)
    x = xbc[:, :d_inner]

    dt = _softplus(dt_ref[...] + dtb_ref[...])
    expand = _head_expand_matrix(nh, M_HEAD_DIM)
    rows8 = lambda v: jnp.broadcast_to(v, (SUBLANES, nh))
    dt_e = _dot_exact_rhs(_split3(rows8(dt)), expand)[0:1, :]
    da_e = _dot_exact_rhs(_split3(rows8(jnp.exp(dt * a_ref[...]))), expand)[0:1, :]
    dtx = x * dt_e

    rowi = lax.broadcasted_iota(jnp.int32, (M_D_STATE, M_D_STATE), 0)
    b_mat = jnp.zeros((M_D_STATE, M_D_STATE), F32)
    c_mat = jnp.zeros((M_D_STATE, M_D_STATE), F32)
    for g in range(M_GROUPS):
        b_row = xbc[:, d_inner + g * M_D_STATE:d_inner + (g + 1) * M_D_STATE]
        c_row = xbc[:, d_inner + gn + g * M_D_STATE:d_inner + gn + (g + 1) * M_D_STATE]
        b_mat = jnp.where(rowi == g, b_row, b_mat)
        c_mat = jnp.where(rowi == g, c_row, c_mat)
    b_cols = b_mat.T
    c_cols = c_mat.T

    ht = st_ref[...].T
    hn_parts, y_parts = [], []
    for g in range(M_GROUPS):
        gs = slice(g * gw, (g + 1) * gw)
        hn = ht[:, gs] * da_e[:, gs] + b_cols[:, g:g + 1] * dtx[:, gs]
        hn_parts.append(hn)
        y_parts.append(jnp.sum(c_cols[:, g:g + 1] * hn, axis=0, keepdims=True))
    hn_ref[...] = jnp.concatenate(hn_parts, axis=1).T
    y = jnp.concatenate(y_parts, axis=1) + de_ref[...] * x
    y_ref[...] = _gate_norm(y, z_ref[...], nw_ref[...], d_inner)


def _m_step(conv_prev, xbc, z, dt, state, w, b, dtb, a, d_e, nw):
    bsz, nprev, cdim = conv_prev.shape
    d_inner = z.shape[1]
    nh = dt.shape[1]
    per_b = lambda r, n: pl.BlockSpec((None, r, n), lambda i: (i, 0, 0))
    return pl.pallas_call(
        functools.partial(_m_step_body, nh=nh, d_inner=d_inner),
        grid=(bsz,),
        in_specs=[per_b(nprev, cdim), per_b(1, cdim), per_b(1, d_inner), per_b(1, nh), per_b(d_inner, M_D_STATE),
                  _resident(w.shape), _resident((1, cdim)), _resident((1, nh)), _resident((1, nh)),
                  _resident((1, d_inner)), _resident((1, d_inner))],
        out_specs=[per_b(1, d_inner), per_b(nprev, cdim), per_b(d_inner, M_D_STATE)],
        out_shape=[jax.ShapeDtypeStruct((bsz, 1, d_inner), F32),
                   jax.ShapeDtypeStruct((bsz, nprev, cdim), F32),
                   jax.ShapeDtypeStruct((bsz, d_inner, M_D_STATE), F32)],
        compiler_params=_cparams("parallel"),
        name="m_step",
    )(conv_prev, xbc.reshape(bsz, 1, cdim), z.reshape(bsz, 1, d_inner), dt.reshape(bsz, 1, nh),
      state.reshape(bsz, d_inner, M_D_STATE), w, b, dtb.reshape(1, nh), a.reshape(1, nh), d_e, nw)


def kernel(x_prompt, x_sample, cache_sb_k, cache_sb_v, state_conv_a, state_mamba_conv, state_ssm, page_table,
           ffn1_norm, ffn1_gate, ffn1_up, ffn1_down, mix_norm, ffn2_norm, ffn2_gate, ffn2_up, ffn2_down,
           final_norm, ab_in_proj, conv_a_w, conv_a_b, conv_a_ln_g, conv_a_ln_b, sb_beta_bias, ab_out_proj,
           m_in_proj, m_conv_w, m_conv_b, m_dt_bias, m_A_log, m_D, m_norm, m_out_proj):
    bp, seq, d = x_prompt.shape
    bs = x_sample.shape[0]
    depth = ffn1_norm.shape[0]
    hd = SB_HEADS * SB_HEAD_DIM
    conv_ch = conv_a_w.shape[2]
    nh = m_dt_bias.shape[1]
    d_inner = nh * M_HEAD_DIM
    conv_dim = m_conv_w.shape[2]

    bf = lambda w: w.astype(BF16)
    f1g, f1u, f1d = bf(ffn1_gate), bf(ffn1_up), bf(ffn1_down)
    f2g, f2u, f2d = bf(ffn2_gate), bf(ffn2_up), bf(ffn2_down)
    ab_in_w, ab_out_w = bf(ab_in_proj), bf(ab_out_proj)
    m_in_w, m_out_w = bf(m_in_proj), bf(m_out_proj)
    fw = final_norm.reshape(1, d)

    xp = x_prompt.reshape(bp * seq, d)
    xs = x_sample.reshape(bs, d)
    outs = {k: [] for k in ("kp", "vp", "cap", "mcp", "ssp", "ks", "vs", "cas", "mcs", "sss")}
    n1 = ffn1_norm.reshape(depth, 1, d)
    n2 = ffn2_norm.reshape(depth, 1, d)
    for layer in range(depth):
        w1 = (n1, f1g, f1u, f1d, fw, layer)
        xp = _ffn(xp, *w1)
        xs = _ffn(xs, *w1)
        mnw = mix_norm[layer].reshape(1, d)
        if layer % 2 == 0:
            e = layer // 2
            cw = (conv_a_w[e], conv_a_b[e].reshape(1, conv_ch), conv_a_ln_g[e].reshape(1, conv_ch),
                  conv_a_ln_b[e].reshape(1, conv_ch))
            wa, wo = ab_out_w[e][:conv_ch], ab_out_w[e][conv_ch:]
            u, qb, k, v, kb, vb = _ab_in(xp, mnw, ab_in_w[e])
            a_out, cvp = _conv_a_prompt(u, bp, *cw)
            o = _sb_attn_prompt(qb, kb, vb, sb_beta_bias[e], bp)
            xp = _out2(xp, a_out, o, wa, wo)
            outs["kp"].append(k.reshape(bp, seq, SB_HEADS, SB_HEAD_DIM))
            outs["vp"].append(v.reshape(bp, seq, SB_HEADS, SB_HEAD_DIM))
            outs["cap"].append(cvp)

            u, qb, k, v, _, _ = _ab_in(xs, mnw, ab_in_w[e])
            a_out, cvs = _conv_a_sample(state_conv_a[e], u, *cw)
            q = qb.astype(F32).reshape(bs, SB_HEADS, SB_HEAD_DIM)
            o = _sb_attn_decode(q, cache_sb_k, cache_sb_v, e, page_table, sb_beta_bias[e])
            xs = _out2(xs, a_out.reshape(bs, conv_ch), o.reshape(bs, hd), wa, wo)
            outs["ks"].append(k.reshape(bs, 1, SB_HEADS, SB_HEAD_DIM))
            outs["vs"].append(v.reshape(bs, 1, SB_HEADS, SB_HEAD_DIM))
            outs["cas"].append(cvs)
        else:
            o = layer // 2
            w_in = m_in_w[o]
            wz, wx, wdt = w_in[:, :d_inner], w_in[:, d_inner:d_inner + conv_dim], w_in[:, d_inner + conv_dim:]
            wdtt = wdt.T
            a = -jnp.exp(m_A_log[o])
            d_e = jnp.repeat(m_D[o], M_HEAD_DIM).reshape(1, d_inner)
            nw = m_norm[o].reshape(1, d_inner)
            cb = m_conv_b[o].reshape(1, conv_dim)

            z, xbc, dt, dtt = _m_in(xp, mnw, wz, wx, wdt, wdtt)
            x, bm, cm, mcp = _m_conv_prompt(xbc, bp, m_conv_w[o], cb, d_inner)
            y, hT = _ssd_prompt(x, bm, cm, z, dt, dtt, m_dt_bias[o], a, d_e, nw, bp)
            xp = _out1(xp, y, m_out_w[o])
            outs["mcp"].append(mcp)
            outs["ssp"].append(hT.reshape(bp, nh, M_HEAD_DIM, M_D_STATE))

            z, xbc, dt, _ = _m_in(xs, mnw, wz, wx, wdt, wdtt)
            y, mcs, hn = _m_step(state_mamba_conv[o], xbc, z, dt, state_ssm[o], m_conv_w[o], cb,
                                 m_dt_bias[o], a, d_e, nw)
            xs = _out1(xs, y.reshape(bs, d_inner), m_out_w[o])
            outs["mcs"].append(mcs)
            outs["sss"].append(hn.reshape(bs, nh, M_HEAD_DIM, M_D_STATE))
        last = layer == depth - 1
        w2 = (n2, f2g, f2u, f2d, fw, layer)
        xp = _ffn(xp, *w2, final=last)
        xs = _ffn(xs, *w2, final=last)
    st = lambda key: jnp.stack(outs[key])
    return (xp.reshape(bp, seq, d), xs.reshape(bs, 1, d),
            st("kp"), st("vp"), st("cap"), st("mcp"), st("ssp"),
            st("ks"), st("vs"), st("cas"), st("mcs"), st("sss"))
```

```python
import functools

import jax
import jax.numpy as jnp
from jax import lax
from jax.experimental import pallas as pl
from jax.experimental.pallas import tpu as pltpu

F32 = jnp.float32
BF16 = jnp.bfloat16
EPS = 1e-6

LANES = 128
SUBLANES = 8
VMEM_LIMIT_BYTES = 56 * 1024 * 1024

SB_HEADS = 8
SB_HEAD_DIM = 64
CONV_W = 31
M_HEAD_DIM = 64
M_GROUPS = 4
M_D_STATE = 128
M_CONV_W = 4
SSD_CHUNK = 128


def _cparams(*sem):
    return pltpu.CompilerParams(dimension_semantics=sem, vmem_limit_bytes=VMEM_LIMIT_BYTES)


def _resident(shape):
    nd = len(shape)
    return pl.BlockSpec(shape, lambda *_: (0,) * nd, pipeline_mode=pl.Buffered(1))


def _resident_layer(shape, layer):
    nd = len(shape)
    return pl.BlockSpec((None,) + tuple(shape[1:]), lambda *_: (layer,) + (0,) * (nd - 1),
                        pipeline_mode=pl.Buffered(1))


def _rms_bf16(x, w):
    y = x * lax.rsqrt(jnp.mean(x * x, axis=-1, keepdims=True) + EPS)
    return (y * w).astype(BF16)


def _dot(a, b):
    return jnp.dot(a, b, preferred_element_type=F32)


def _dot_nt(a, b):
    return lax.dot_general(a, b, (((1,), (1,)), ((), ())), preferred_element_type=F32)


def _split2(v):
    hi = v.astype(BF16)
    lo = (v - hi.astype(F32)).astype(BF16)
    return hi, lo


def _split3(v):
    hi = v.astype(BF16)
    r = v - hi.astype(F32)
    mid = r.astype(BF16)
    lo = (r - mid.astype(F32)).astype(BF16)
    return hi, mid, lo


def _dot_exact_rhs(parts, m):
    acc = _dot(parts[0], m)
    for p in parts[1:]:
        acc = acc + _dot(p, m)
    return acc


def _softplus(z):
    return jnp.maximum(z, 0.0) + jnp.log1p(jnp.exp(-jnp.abs(z)))


def _silu(v):
    return v * jax.nn.sigmoid(v)


def _token_tile(t, want):
    return want if t % want == 0 else t


def _ffn_body(x_ref, nw_ref, wg_ref, wu_ref, wd_ref, fw_ref, o_ref, *, final):
    x = x_ref[...]
    h = _rms_bf16(x, nw_ref[...])
    g = _dot(h, wg_ref[...])
    u = _dot(h, wu_ref[...])
    a = (_silu(g) * u).astype(BF16)
    y = x + 0.5 * _dot(a, wd_ref[...])
    if final:
        y = y * lax.rsqrt(jnp.mean(y * y, axis=-1, keepdims=True) + EPS) * fw_ref[...]
    o_ref[...] = y


def _ffn(x, nw, wg, wu, wd, fw, layer, *, final=False):
    t, d = x.shape
    tm = _token_tile(t, 512)
    return pl.pallas_call(
        functools.partial(_ffn_body, final=final),
        grid=(t // tm,),
        in_specs=[pl.BlockSpec((tm, d), lambda i: (i, 0)),
                  _resident_layer(nw.shape, layer), _resident_layer(wg.shape, layer),
                  _resident_layer(wu.shape, layer), _resident_layer(wd.shape, layer),
                  _resident((1, d))],
        out_specs=pl.BlockSpec((tm, d), lambda i: (i, 0)),
        out_shape=jax.ShapeDtypeStruct((t, d), F32),
        compiler_params=_cparams("parallel"),
        name="ffn",
    )(x, nw, wg, wu, wd, fw)


def _ab_in_body(x_ref, nw_ref, w_ref, u_ref, qb_ref, k_ref, v_ref, kb_ref, vb_ref, *, c, hd):
    h = _rms_bf16(x_ref[...], nw_ref[...])
    p = _dot(h, w_ref[...])
    u_ref[...] = p[:, :c] * jax.nn.sigmoid(p[:, c:2 * c])
    qb_ref[...] = (p[:, 2 * c:2 * c + hd] * (SB_HEAD_DIM ** -0.5)).astype(BF16)
    k = p[:, 2 * c + hd:2 * c + 2 * hd]
    v = p[:, 2 * c + 2 * hd:]
    k_ref[...] = k
    v_ref[...] = v
    kb_ref[...] = k.astype(BF16)
    vb_ref[...] = v.astype(BF16)


def _ab_in(x, nw, w):
    t, d = x.shape
    hd = SB_HEADS * SB_HEAD_DIM
    c = (w.shape[1] - 3 * hd) // 2
    tm = _token_tile(t, 512)
    row = lambda n: pl.BlockSpec((tm, n), lambda i: (i, 0))
    return pl.pallas_call(
        functools.partial(_ab_in_body, c=c, hd=hd),
        grid=(t // tm,),
        in_specs=[row(d), _resident((1, d)), _resident(w.shape)],
        out_specs=[row(c), row(hd), row(hd), row(hd), row(hd), row(hd)],
        out_shape=[jax.ShapeDtypeStruct((t, c), F32), jax.ShapeDtypeStruct((t, hd), BF16),
                   jax.ShapeDtypeStruct((t, hd), F32), jax.ShapeDtypeStruct((t, hd), F32),
                   jax.ShapeDtypeStruct((t, hd), BF16), jax.ShapeDtypeStruct((t, hd), BF16)],
        compiler_params=_cparams("parallel"),
        name="ab_in",
    )(x, nw, w)


_CONV_ROWS = 32
_CONV_HALO = 32


def _layernorm_silu(cv, g, b):
    mu = jnp.mean(cv, axis=-1, keepdims=True)
    d = cv - mu
    var = jnp.mean(d * d, axis=-1, keepdims=True)
    return _silu(d * lax.rsqrt(var + EPS) * g + b)


def _conv_a_body(u_ref, w_ref, b_ref, g_ref, be_ref, a_ref, cn_ref, ext_ref, *, tt):
    t = pl.program_id(1)

    @pl.when(t == 0)
    def _():
        ext_ref[0:_CONV_HALO, :] = jnp.zeros((_CONV_HALO, ext_ref.shape[1]), F32)

    ext_ref[_CONV_HALO:_CONV_HALO + tt, :] = u_ref[...]
    first = _CONV_HALO - (CONV_W - 1)
    for r in range(tt // _CONV_ROWS):
        base = r * _CONV_ROWS + first
        acc = w_ref[0:1, :] * ext_ref[base:base + _CONV_ROWS, :]
        for j in range(1, CONV_W):
            acc = acc + w_ref[j:j + 1, :] * ext_ref[base + j:base + j + _CONV_ROWS, :]
        cv = acc + b_ref[...]
        a_ref[r * _CONV_ROWS:(r + 1) * _CONV_ROWS, :] = _layernorm_silu(cv, g_ref[...], be_ref[...]).astype(BF16)

    @pl.when(t == pl.num_programs(1) - 1)
    def _():
        cn_ref[...] = ext_ref[_CONV_HALO + tt - (CONV_W - 1):_CONV_HALO + tt, :]

    ext_ref[0:_CONV_HALO, :] = ext_ref[tt:tt + _CONV_HALO, :]


def _conv_a_prompt(u, bsz, w, b, g, be):
    t_all, c = u.shape
    t = t_all // bsz
    tt = _token_tile(t, 512)
    nt = t // tt
    return pl.pallas_call(
        functools.partial(_conv_a_body, tt=tt),
        grid=(bsz, nt),
        in_specs=[pl.BlockSpec((tt, c), lambda i, j: (i * nt + j, 0)),
                  _resident(w.shape), _resident((1, c)), _resident((1, c)), _resident((1, c))],
        out_specs=[pl.BlockSpec((tt, c), lambda i, j: (i * nt + j, 0)),
                   pl.BlockSpec((None, CONV_W - 1, c), lambda i, j: (i, 0, 0))],
        out_shape=[jax.ShapeDtypeStruct((t_all, c), BF16),
                   jax.ShapeDtypeStruct((bsz, CONV_W - 1, c), F32)],
        scratch_shapes=[pltpu.VMEM((_CONV_HALO + tt, c), F32)],
        compiler_params=_cparams("parallel", "arbitrary"),
        name="conv_a_prompt",
    )(u, w, b, g, be)


def _conv_a_sample_body(prev_ref, u_ref, w_ref, b_ref, g_ref, be_ref, a_ref, cn_ref):
    nprev = CONV_W - 1
    prev = prev_ref[...]
    u = u_ref[...]
    cv = jnp.sum(w_ref[0:nprev, :] * prev, axis=0, keepdims=True) + w_ref[nprev:CONV_W, :] * u + b_ref[...]
    a_ref[...] = _layernorm_silu(cv, g_ref[...], be_ref[...])
    cn_ref[0:nprev - 1, :] = prev_ref[1:nprev, :]
    cn_ref[nprev - 1:nprev, :] = u


def _conv_a_sample(prev, u, w, b, g, be):
    bsz, nprev, c = prev.shape
    return pl.pallas_call(
        _conv_a_sample_body,
        grid=(bsz,),
        in_specs=[pl.BlockSpec((None, nprev, c), lambda i: (i, 0, 0)),
                  pl.BlockSpec((None, 1, c), lambda i: (i, 0, 0)),
                  _resident(w.shape), _resident((1, c)), _resident((1, c)), _resident((1, c))],
        out_specs=[pl.BlockSpec((None, 1, c), lambda i: (i, 0, 0)),
                   pl.BlockSpec((None, nprev, c), lambda i: (i, 0, 0))],
        out_shape=[jax.ShapeDtypeStruct((bsz, 1, c), F32),
                   jax.ShapeDtypeStruct((bsz, nprev, c), F32)],
        compiler_params=_cparams("parallel"),
        name="conv_a_sample",
    )(prev, u.reshape(bsz, 1, c), w, b, g, be)


LOG2E = 1.4426950408889634


def _suffix_matrix(tk, with_total):
    cols = tk + LANES if with_total else tk
    j = lax.broadcasted_iota(jnp.int32, (tk, cols), 0)
    s = lax.broadcasted_iota(jnp.int32, (tk, cols), 1)
    return jnp.where((j > s) | (s >= tk), 1.0, 0.0).astype(BF16)


def _neg_abs(x):
    return lax.bitcast_convert_type(lax.bitcast_convert_type(x, jnp.uint32) | jnp.uint32(0x80000000), F32)


_SB_STRIP = 32


def _sb_block(z, bias2, v_masked, sufm, carry, diagonal):
    tq, tk = z.shape
    strips = [slice(r * _SB_STRIP, (r + 1) * _SB_STRIP) for r in range(tq // _SB_STRIP)]

    def admitted(r):
        rowi = lax.broadcasted_iota(jnp.int32, (_SB_STRIP, tk), 0) + r * _SB_STRIP
        coli = lax.broadcasted_iota(jnp.int32, (_SB_STRIP, tk), 1)
        return coli < rowi

    sp_parts, arg_parts, first = [], [], []
    for r, rows in enumerate(strips):
        zs = z[rows] * LOG2E + bias2
        sp = jnp.maximum(zs, 0.0) + jnp.log2(1.0 + jnp.exp2(_neg_abs(zs)))
        arg_parts.append(zs - sp)
        if diagonal:
            sp = jnp.where(admitted(r), sp, 0.0)
        first.append(sp[:, 0:1])
        sp_parts.append(sp.astype(BF16))
    sums = _dot(jnp.concatenate(sp_parts, axis=0), sufm)
    w_parts = []
    for r, rows in enumerate(strips):
        cr = carry[rows]
        after = jnp.concatenate([sums[rows, c * LANES:(c + 1) * LANES] + cr for c in range(tk // LANES)], axis=1)
        wgt = jnp.exp2(arg_parts[r] - after)
        if diagonal:
            wgt = jnp.where(admitted(r), wgt, 0.0)
        w_parts.append(wgt.astype(BF16))
    out = _dot(jnp.concatenate(w_parts, axis=0), v_masked)
    total = sums[:, 0:1] + jnp.concatenate(first, axis=0)
    return out, carry + jnp.broadcast_to(total, carry.shape)


def _sb_prompt_body(bias_ref, q_ref, k_ref, v_ref, o_ref, qm_ref, acc_ref, carry_ref, *, tq):
    i = pl.program_id(1)
    tk = tq
    sufm = _suffix_matrix(tk, False)
    lane = lax.broadcasted_iota(jnp.int32, (1, LANES), 1)
    sels = [lane < SB_HEAD_DIM, lane >= SB_HEAD_DIM]
    for head in range(SB_HEADS):
        qp = q_ref[:, (head // 2) * LANES:(head // 2 + 1) * LANES]
        qm_ref[head] = jnp.where(sels[head % 2], qp, jnp.zeros_like(qp))
    acc_ref[...] = jnp.zeros_like(acc_ref)
    carry_ref[...] = jnp.zeros_like(carry_ref)

    def block(kj, diagonal):
        start = pl.multiple_of(kj * tk, tk)
        for hp in range(SB_HEADS // 2):
            lanes = slice(hp * LANES, (hp + 1) * LANES)
            kb = k_ref[pl.ds(start, tk), lanes]
            vb = v_ref[pl.ds(start, tk), lanes]
            out = None
            for hh in range(2):
                head = 2 * hp + hh
                vm = jnp.where(sels[hh], vb, jnp.zeros_like(vb))
                o, c2 = _sb_block(_dot_nt(qm_ref[head], kb), bias_ref[head] * LOG2E, vm, sufm,
                                  carry_ref[head], diagonal)
                carry_ref[head] = c2
                out = o if out is None else out + o
            acc_ref[:, lanes] += out

    block(i, True)

    def step(jj, _):
        block(i - jj, False)
        return 0

    lax.fori_loop(1, i + 1, step, 0)
    o_ref[...] = acc_ref[...].astype(o_ref.dtype)


def _sb_attn_prompt(qb, kb, vb, bias, bsz):
    t_all, hd = qb.shape
    t = t_all // bsz
    tq = _token_tile(t, 256)
    nq = t // tq
    return pl.pallas_call(
        functools.partial(_sb_prompt_body, tq=tq),
        grid=(bsz, nq),
        in_specs=[pl.BlockSpec(memory_space=pltpu.SMEM),
                  pl.BlockSpec((tq, hd), lambda b, i: (b * nq + i, 0)),
                  pl.BlockSpec((t, hd), lambda b, i: (b, 0)),
                  pl.BlockSpec((t, hd), lambda b, i: (b, 0))],
        out_specs=pl.BlockSpec((tq, hd), lambda b, i: (b * nq + i, 0)),
        out_shape=jax.ShapeDtypeStruct((t_all, hd), BF16),
        scratch_shapes=[pltpu.VMEM((SB_HEADS, tq, LANES), BF16), pltpu.VMEM((tq, hd), F32),
                        pltpu.VMEM((SB_HEADS, tq, LANES), F32)],
        compiler_params=_cparams("parallel", "arbitrary"),
        name="sb_attn_prompt",
    )(bias, qb, kb, vb)


_DEC_PAGES = 16


def _sb_decode_body(pt_ref, bias_ref, q_ref, *refs, page, npg):
    k_refs, v_refs = refs[:npg], refs[npg:2 * npg]
    o_ref, acc_ref, carry_ref = refs[2 * npg:]
    p = pl.program_id(1)
    hd = SB_HEADS * SB_HEAD_DIM

    @pl.when(p == 0)
    def _():
        acc_ref[...] = jnp.zeros_like(acc_ref)
        carry_ref[...] = jnp.zeros_like(carry_ref)

    own = (lax.broadcasted_iota(jnp.int32, (SB_HEADS, hd), 1) // SB_HEAD_DIM
           == lax.broadcasted_iota(jnp.int32, (SB_HEADS, hd), 0))
    q8 = jnp.where(own, jnp.broadcast_to(q_ref[...], (SB_HEADS, hd)), 0.0).astype(BF16)
    sufm = _suffix_matrix(page, True)
    args, pieces = [], []
    for g in range(npg):
        kt = k_refs[g][...].reshape(hd, page).astype(BF16)
        z = _dot(q8, kt) + bias_ref[...]
        sp = _softplus(z)
        args.append(z - sp)
        pieces.append(sp)
    sums = _dot_exact_rhs(_split2(jnp.concatenate(pieces, axis=0)), sufm)
    acc = acc_ref[...]
    run = carry_ref[...]
    for g in range(npg):
        rows = slice(g * SB_HEADS, (g + 1) * SB_HEADS)
        wgt = jnp.exp(args[g] - (sums[rows, :page] + run))
        run = run + sums[rows, page:]
        vt = v_refs[g][...].reshape(hd, page).astype(BF16)
        acc = acc + _dot_nt(wgt.astype(BF16), vt)
    acc_ref[...] = acc
    carry_ref[...] = run

    @pl.when(p == pl.num_programs(1) - 1)
    def _():
        o_ref[...] = jnp.sum(jnp.where(own, acc, 0.0), axis=0, keepdims=True)


def _sb_attn_decode(q, cache_kt, cache_vt, layer, page_table, bias):
    bsz, hd = q.shape
    n_pages = page_table.shape[1]
    page = cache_kt.shape[4]
    assert page == LANES
    npg = _DEC_PAGES if n_pages % _DEC_PAGES == 0 else 1

    def page_spec(g):
        return pl.BlockSpec((None, None, SB_HEADS, SB_HEAD_DIM, page),
                            lambda b, p, pt: (layer, pt[b * n_pages + n_pages - 1 - (p * npg + g)], 0, 0, 0))

    per_b = pl.BlockSpec((None, 1, hd), lambda b, p, pt: (b, 0, 0))
    grid_spec = pltpu.PrefetchScalarGridSpec(
        num_scalar_prefetch=1,
        grid=(bsz, n_pages // npg),
        in_specs=[pl.BlockSpec((SB_HEADS, 1), lambda b, p, pt: (0, 0)), per_b]
                 + [page_spec(g) for g in range(npg)] * 2,
        out_specs=per_b,
        scratch_shapes=[pltpu.VMEM((SB_HEADS, hd), F32), pltpu.VMEM((SB_HEADS, LANES), F32)],
    )
    return pl.pallas_call(
        functools.partial(_sb_decode_body, page=page, npg=npg),
        grid_spec=grid_spec,
        out_shape=jax.ShapeDtypeStruct((bsz, 1, hd), F32),
        compiler_params=_cparams("parallel", "arbitrary"),
        name="sb_attn_decode",
    )(page_table.reshape(-1), bias.reshape(SB_HEADS, 1), q.reshape(bsz, 1, hd),
      *([cache_kt] * npg), *([cache_vt] * npg))


def _out2_body(x_ref, a_ref, o_ref, wa_ref, wo_ref, y_ref):
    y_ref[...] = (x_ref[...] + _dot(a_ref[...].astype(BF16), wa_ref[...])
                  + _dot(o_ref[...].astype(BF16), wo_ref[...]))


def _out2(x, a, o, wa, wo):
    t, d = x.shape
    tm = _token_tile(t, 512)
    row = lambda n: pl.BlockSpec((tm, n), lambda i: (i, 0))
    return pl.pallas_call(
        _out2_body,
        grid=(t // tm,),
        in_specs=[row(d), row(a.shape[1]), row(o.shape[1]), _resident(wa.shape), _resident(wo.shape)],
        out_specs=row(d),
        out_shape=jax.ShapeDtypeStruct((t, d), F32),
        compiler_params=_cparams("parallel"),
        name="ab_out",
    )(x, a, o, wa, wo)


def _out1_body(x_ref, y_ref, w_ref, o_ref):
    o_ref[...] = x_ref[...] + _dot(y_ref[...].astype(BF16), w_ref[...])


def _out1(x, y, w):
    t, d = x.shape
    tm = _token_tile(t, 512)
    row = lambda n: pl.BlockSpec((tm, n), lambda i: (i, 0))
    return pl.pallas_call(
        _out1_body,
        grid=(t // tm,),
        in_specs=[row(d), row(y.shape[1]), _resident(w.shape)],
        out_specs=row(d),
        out_shape=jax.ShapeDtypeStruct((t, d), F32),
        compiler_params=_cparams("parallel"),
        name="m_out",
    )(x, y, w)


def _m_in_body(x_ref, nw_ref, wz_ref, wx_ref, wdt_ref, wdtt_ref, z_ref, xbc_ref, dt_ref, dtt_ref):
    h = _rms_bf16(x_ref[...], nw_ref[...])
    z_ref[...] = _dot(h, wz_ref[...])
    xbc_ref[...] = _dot(h, wx_ref[...])
    dt_ref[...] = _dot(h, wdt_ref[...])
    dtt_ref[...] = _dot_nt(wdtt_ref[...], h)


def _m_in(x, nw, wz, wx, wdt, wdtt):
    t, d = x.shape
    tm = _token_tile(t, 512)
    nh = wdt.shape[1]
    row = lambda n: pl.BlockSpec((tm, n), lambda i: (i, 0))
    return pl.pallas_call(
        _m_in_body,
        grid=(t // tm,),
        in_specs=[row(d), _resident((1, d)), _resident(wz.shape), _resident(wx.shape),
                  _resident(wdt.shape), _resident(wdtt.shape)],
        out_specs=[row(wz.shape[1]), row(wx.shape[1]), row(nh), pl.BlockSpec((nh, tm), lambda i: (0, i))],
        out_shape=[jax.ShapeDtypeStruct((t, wz.shape[1]), F32), jax.ShapeDtypeStruct((t, wx.shape[1]), F32),
                   jax.ShapeDtypeStruct((t, nh), F32), jax.ShapeDtypeStruct((nh, t), F32)],
        compiler_params=_cparams("parallel"),
        name="m_in",
    )(x, nw, wz, wx, wdt, wdtt)


_MCONV_LANES = 512


def _m_conv_body(xbc_ref, w_ref, b_ref, x_ref, bm_ref, cm_ref, cn_ref, ext_ref, *, tt, d_inner, gn):
    t = pl.program_id(1)
    halo = SUBLANES
    nprev = M_CONV_W - 1

    @pl.when(t == 0)
    def _():
        ext_ref[0:halo, :] = jnp.zeros((halo, ext_ref.shape[1]), F32)

    ext_ref[halo:halo + tt, :] = xbc_ref[...]
    first = halo - nprev
    cdim = ext_ref.shape[1]
    for lc in range(cdim // _MCONV_LANES):
        ls = slice(lc * _MCONV_LANES, (lc + 1) * _MCONV_LANES)
        for r in range(tt // _CONV_ROWS):
            base = r * _CONV_ROWS + first
            acc = w_ref[0:1, ls] * ext_ref[base:base + _CONV_ROWS, ls]
            for j in range(1, M_CONV_W):
                acc = acc + w_ref[j:j + 1, ls] * ext_ref[base + j:base + j + _CONV_ROWS, ls]
            val = _silu(acc + b_ref[:, ls])
            rows = slice(r * _CONV_ROWS, (r + 1) * _CONV_ROWS)
            lo = lc * _MCONV_LANES
            if lo < d_inner:
                x_ref[rows, lo:lo + _MCONV_LANES] = val
            elif lo < d_inner + gn:
                bm_ref[rows, lo - d_inner:lo - d_inner + _MCONV_LANES] = val
            else:
                cm_ref[rows, lo - d_inner - gn:lo - d_inner - gn + _MCONV_LANES] = val.astype(BF16)

    @pl.when(t == pl.num_programs(1) - 1)
    def _():
        cn_ref[...] = ext_ref[halo + tt - nprev:halo + tt, :]

    ext_ref[0:halo, :] = ext_ref[tt:tt + halo, :]


def _m_conv_prompt(xbc, bsz, w, b, d_inner):
    t_all, cdim = xbc.shape
    gn = M_GROUPS * M_D_STATE
    assert gn == _MCONV_LANES and d_inner % _MCONV_LANES == 0
    t = t_all // bsz
    tt = _token_tile(t, 256)
    nt = t // tt
    row = lambda n: pl.BlockSpec((tt, n), lambda i, j: (i * nt + j, 0))
    return pl.pallas_call(
        functools.partial(_m_conv_body, tt=tt, d_inner=d_inner, gn=gn),
        grid=(bsz, nt),
        in_specs=[row(cdim), _resident(w.shape), _resident((1, cdim))],
        out_specs=[row(d_inner), row(gn), row(gn),
                   pl.BlockSpec((None, M_CONV_W - 1, cdim), lambda i, j: (i, 0, 0))],
        out_shape=[jax.ShapeDtypeStruct((t_all, d_inner), F32), jax.ShapeDtypeStruct((t_all, gn), F32),
                   jax.ShapeDtypeStruct((t_all, gn), BF16),
                   jax.ShapeDtypeStruct((bsz, M_CONV_W - 1, cdim), F32)],
        scratch_shapes=[pltpu.VMEM((SUBLANES + tt, cdim), F32)],
        compiler_params=_cparams("parallel", "arbitrary"),
        name="m_conv_prompt",
    )(xbc, w, b)


def _head_expand_matrix(nh, width):
    h = lax.broadcasted_iota(jnp.int32, (nh, nh * width), 0)
    c = lax.broadcasted_iota(jnp.int32, (nh, nh * width), 1)
    return jnp.where(c // width == h, 1.0, 0.0).astype(BF16)


def _gate_norm(y, z, nw, d_inner):
    y = y * _silu(z)
    gw = d_inner // M_GROUPS
    parts = []
    for g in range(M_GROUPS):
        yg = y[:, g * gw:(g + 1) * gw]
        parts.append(yg * lax.rsqrt(jnp.mean(yg * yg, axis=-1, keepdims=True) + EPS))
    return jnp.concatenate(parts, axis=1) * nw


def _ssd_body(x_ref, bm_ref, cm_ref, z_ref, dt_ref, dtt_ref, dtb_ref, dtbt_ref, a_ref, at_ref, de_ref, nw_ref,
              y_ref, hT_ref, h_ref, yd_ref, *, nh, d_inner):
    c = pl.program_id(1)
    L = SSD_CHUNK
    hpg = nh // M_GROUPS
    gw = hpg * M_HEAD_DIM

    @pl.when(c == 0)
    def _():
        h_ref[...] = jnp.zeros_like(h_ref)

    dt = _softplus(dt_ref[...] + dtb_ref[...])
    dtt = _softplus(dtt_ref[...] + dtbt_ref[...])
    row = lax.broadcasted_iota(jnp.int32, (L, L), 0)
    col = lax.broadcasted_iota(jnp.int32, (L, L), 1)
    causal = col <= row
    tri = jnp.where(causal, 1.0, 0.0).astype(BF16)
    trit = jnp.where(row <= col, 1.0, 0.0).astype(BF16)
    dta_hi, dta_lo = _split2(dt * a_ref[...])
    acs = _dot(tri, dta_hi) + _dot(tri, dta_lo)
    acst = _dot_exact_rhs(_split2(dtt * at_ref[...]), trit)
    acs_last = acs[L - 1:L, :]
    expand = _head_expand_matrix(nh, M_HEAD_DIM)
    dt_e = _dot_exact_rhs(_split3(dt), expand)
    eacs_e = _dot_exact_rhs(_split3(jnp.exp(acs)), expand)
    eend_e = _dot_exact_rhs(_split3(jnp.exp(acs_last - acs)), expand)
    etot_e = _dot_exact_rhs(_split3(jnp.broadcast_to(jnp.exp(acs_last), (SUBLANES, nh))), expand)[0:1, :]

    x = x_ref[...]
    dtx = x * dt_e
    dtx_b = dtx.astype(BF16)
    dtxe_b = (dtx * eend_e).astype(BF16)
    for g in range(M_GROUPS):
        ns = slice(g * M_D_STATE, (g + 1) * M_D_STATE)
        gs = slice(g * gw, (g + 1) * gw)
        bg = bm_ref[:, ns]
        cg = cm_ref[:, ns]
        cb = _dot_nt(cg, bg.astype(BF16))
        hg = h_ref[:, gs]
        yd_ref[:, gs] = _dot(cg, hg.astype(BF16)) * eacs_e[:, gs]
        h_ref[:, gs] = hg * etot_e[:, gs] + _dot(bg.T.astype(BF16), dtxe_b[:, gs])
        for hh in range(hpg):
            head = g * hpg + hh
            hs = slice(head * M_HEAD_DIM, (head + 1) * M_HEAD_DIM)
            dec = jnp.exp(jnp.where(causal, acs[:, head:head + 1] - acst[head:head + 1, :], -jnp.inf))
            yd_ref[:, hs] += _dot((cb * dec).astype(BF16), dtx_b[:, hs])

    y = yd_ref[...] + de_ref[...] * x
    y_ref[...] = _gate_norm(y, z_ref[...], nw_ref[...], d_inner).astype(BF16)

    @pl.when(c == pl.num_programs(1) - 1)
    def _():
        hT_ref[...] = h_ref[...].T


def _ssd_prompt(x, bm, cm, z, dt, dtt, dtb, a, d_e, nw, bsz):
    t_all, d_inner = x.shape
    nh = dt.shape[1]
    gn = bm.shape[1]
    t = t_all // bsz
    L = SSD_CHUNK
    assert t % L == 0
    nc = t // L
    row = lambda n: pl.BlockSpec((L, n), lambda b, c: (b * nc + c, 0))
    return pl.pallas_call(
        functools.partial(_ssd_body, nh=nh, d_inner=d_inner),
        grid=(bsz, nc),
        in_specs=[row(d_inner), row(gn), row(gn), row(d_inner), row(nh),
                  pl.BlockSpec((nh, L), lambda b, c: (0, b * nc + c)),
                  _resident((1, nh)), _resident((nh, 1)), _resident((1, nh)), _resident((nh, 1)),
                  _resident((1, d_inner)), _resident((1, d_inner))],
        out_specs=[row(d_inner), pl.BlockSpec((None, d_inner, M_D_STATE), lambda b, c: (b, 0, 0))],
        out_shape=[jax.ShapeDtypeStruct((t_all, d_inner), BF16),
                   jax.ShapeDtypeStruct((bsz, d_inner, M_D_STATE), F32)],
        scratch_shapes=[pltpu.VMEM((M_D_STATE, d_inner), F32), pltpu.VMEM((L, d_inner), F32)],
        compiler_params=_cparams("parallel", "arbitrary"),
        name="ssd_prompt",
    )(x, bm, cm, z, dt, dtt, dtb.reshape(1, nh), dtb.reshape(nh, 1), a.reshape(1, nh), a.reshape(nh, 1), d_e, nw)


def _m_step_body(cprev_ref, xbc_ref, z_ref, dt_ref, st_ref, w_ref, b_ref, dtb_ref, a_ref, de_ref, nw_ref,
                 y_ref, cn_ref, hn_ref, *, nh, d_inner):
    nprev = M_CONV_W - 1
    gn = M_GROUPS * M_D_STATE
    gw = d_inner // M_GROUPS
    new = xbc_ref[...]
    acc = w_ref[nprev:M_CONV_W, :] * new + b_ref[...]
    for j in range(nprev):
        acc = acc + w_ref[j:j + 1, :] * cprev_ref[j:j + 1, :]
    cn_ref[0:nprev - 1, :] = cprev_ref[1:nprev, :]
    cn_ref[nprev - 1:nprev, :] = new
    xbc = _silu(acc)
    x = xbc[:, :d_inner]

    dt = _softplus(dt_ref[...] + dtb_ref[...])
    expand = _head_expand_matrix(nh, M_HEAD_DIM)
    rows8 = lambda v: jnp.broadcast_to(v, (SUBLANES, nh))
    dt_e = _dot_exact_rhs(_split3(rows8(dt)), expand)[0:1, :]
    da_e = _dot_exact_rhs(_split3(rows8(jnp.exp(dt * a_ref[...]))), expand)[0:1, :]
    dtx = x * dt_e

    rowi = lax.broadcasted_iota(jnp.int32, (M_D_STATE, M_D_STATE), 0)
    b_mat = jnp.zeros((M_D_STATE, M_D_STATE), F32)
    c_mat = jnp.zeros((M_D_STATE, M_D_STATE), F32)
    for g in range(M_GROUPS):
        b_row = xbc[:, d_inner + g * M_D_STATE:d_inner + (g + 1) * M_D_STATE]
        c_row = xbc[:, d_inner + gn + g * M_D_STATE:d_inner + gn + (g + 1) * M_D_STATE]
        b_mat = jnp.where(rowi == g, b_row, b_mat)
        c_mat = jnp.where(rowi == g, c_row, c_mat)
    b_cols = b_mat.T
    c_cols = c_mat.T

    ht = st_ref[...].T
    hn_parts, y_parts = [], []
    for g in range(M_GROUPS):
        gs = slice(g * gw, (g + 1) * gw)
        hn = ht[:, gs] * da_e[:, gs] + b_cols[:, g:g + 1] * dtx[:, gs]
        hn_parts.append(hn)
        y_parts.append(jnp.sum(c_cols[:, g:g + 1] * hn, axis=0, keepdims=True))
    hn_ref[...] = jnp.concatenate(hn_parts, axis=1).T
    y = jnp.concatenate(y_parts, axis=1) + de_ref[...] * x
    y_ref[...] = _gate_norm(y, z_ref[...], nw_ref[...], d_inner)


def _m_step(conv_prev, xbc, z, dt, state, w, b, dtb, a, d_e, nw):
    bsz, nprev, cdim = conv_prev.shape
    d_inner = z.shape[1]
    nh = dt.shape[1]
    per_b = lambda r, n: pl.BlockSpec((None, r, n), lambda i: (i, 0, 0))
    return pl.pallas_call(
        functools.partial(_m_step_body, nh=nh, d_inner=d_inner),
        grid=(bsz,),
        in_specs=[per_b(nprev, cdim), per_b(1, cdim), per_b(1, d_inner), per_b(1, nh), per_b(d_inner, M_D_STATE),
                  _resident(w.shape), _resident((1, cdim)), _resident((1, nh)), _resident((1, nh)),
                  _resident((1, d_inner)), _resident((1, d_inner))],
        out_specs=[per_b(1, d_inner), per_b(nprev, cdim), per_b(d_inner, M_D_STATE)],
        out_shape=[jax.ShapeDtypeStruct((bsz, 1, d_inner), F32),
                   jax.ShapeDtypeStruct((bsz, nprev, cdim), F32),
                   jax.ShapeDtypeStruct((bsz, d_inner, M_D_STATE), F32)],
        compiler_params=_cparams("parallel"),
        name="m_step",
    )(conv_prev, xbc.reshape(bsz, 1, cdim), z.reshape(bsz, 1, d_inner), dt.reshape(bsz, 1, nh),
      state.reshape(bsz, d_inner, M_D_STATE), w, b, dtb.reshape(1, nh), a.reshape(1, nh), d_e, nw)


def kernel(x_prompt, x_sample, cache_sb_k, cache_sb_v, state_conv_a, state_mamba_conv, state_ssm, page_table,
           ffn1_norm, ffn1_gate, ffn1_up, ffn1_down, mix_norm, ffn2_norm, ffn2_gate, ffn2_up, ffn2_down,
           final_norm, ab_in_proj, conv_a_w, conv_a_b, conv_a_ln_g, conv_a_ln_b, sb_beta_bias, ab_out_proj,
           m_in_proj, m_conv_w, m_conv_b, m_dt_bias, m_A_log, m_D, m_norm, m_out_proj):
    bp, seq, d = x_prompt.shape
    bs = x_sample.shape[0]
    depth = ffn1_norm.shape[0]
    hd = SB_HEADS * SB_HEAD_DIM
    conv_ch = conv_a_w.shape[2]
    nh = m_dt_bias.shape[1]
    d_inner = nh * M_HEAD_DIM
    conv_dim = m_conv_w.shape[2]

    bf = lambda w: w.astype(BF16)
    f1g, f1u, f1d = bf(ffn1_gate), bf(ffn1_up), bf(ffn1_down)
    f2g, f2u, f2d = bf(ffn2_gate), bf(ffn2_up), bf(ffn2_down)
    ab_in_w, ab_out_w = bf(ab_in_proj), bf(ab_out_proj)
    m_in_w, m_out_w = bf(m_in_proj), bf(m_out_proj)
    fw = final_norm.reshape(1, d)
    cache_kt = jnp.transpose(cache_sb_k, (0, 1, 3, 4, 2))
    cache_vt = jnp.transpose(cache_sb_v, (0, 1, 3, 4, 2))

    xp = x_prompt.reshape(bp * seq, d)
    xs = x_sample.reshape(bs, d)
    outs = {k: [] for k in ("kp", "vp", "cap", "mcp", "ssp", "ks", "vs", "cas", "mcs", "sss")}
    n1 = ffn1_norm.reshape(depth, 1, d)
    n2 = ffn2_norm.reshape(depth, 1, d)
    for layer in range(depth):
        w1 = (n1, f1g, f1u, f1d, fw, layer)
        xp = _ffn(xp, *w1)
        xs = _ffn(xs, *w1)
        mnw = mix_norm[layer].reshape(1, d)
        if layer % 2 == 0:
            e = layer // 2
            cw = (conv_a_w[e], conv_a_b[e].reshape(1, conv_ch), conv_a_ln_g[e].reshape(1, conv_ch),
                  conv_a_ln_b[e].reshape(1, conv_ch))
            wa, wo = ab_out_w[e][:conv_ch], ab_out_w[e][conv_ch:]
            u, qb, k, v, kb, vb = _ab_in(xp, mnw, ab_in_w[e])
            a_out, cvp = _conv_a_prompt(u, bp, *cw)
            o = _sb_attn_prompt(qb, kb, vb, sb_beta_bias[e], bp)
            xp = _out2(xp, a_out, o, wa, wo)
            outs["kp"].append(k.reshape(bp, seq, SB_HEADS, SB_HEAD_DIM))
            outs["vp"].append(v.reshape(bp, seq, SB_HEADS, SB_HEAD_DIM))
            outs["cap"].append(cvp)

            u, qb, k, v, _, _ = _ab_in(xs, mnw, ab_in_w[e])
            a_out, cvs = _conv_a_sample(state_conv_a[e], u, *cw)
            o = _sb_attn_decode(qb.astype(F32), cache_kt, cache_vt, e, page_table, sb_beta_bias[e])
            xs = _out2(xs, a_out.reshape(bs, conv_ch), o.reshape(bs, hd), wa, wo)
            outs["ks"].append(k.reshape(bs, 1, SB_HEADS, SB_HEAD_DIM))
            outs["vs"].append(v.reshape(bs, 1, SB_HEADS, SB_HEAD_DIM))
            outs["cas"].append(cvs)
        else:
            o = layer // 2
            w_in = m_in_w[o]
            wz, wx, wdt = w_in[:, :d_inner], w_in[:, d_inner:d_inner + conv_dim], w_in[:, d_inner + conv_dim:]
            wdtt = wdt.T
            a = -jnp.exp(m_A_log[o])
            d_e = jnp.repeat(m_D[o], M_HEAD_DIM).reshape(1, d_inner)
            nw = m_norm[o].reshape(1, d_inner)
            cb = m_conv_b[o].reshape(1, conv_dim)

            z, xbc, dt, dtt = _m_in(xp, mnw, wz, wx, wdt, wdtt)
            x, bm, cm, mcp = _m_conv_prompt(xbc, bp, m_conv_w[o], cb, d_inner)
            y, hT = _ssd_prompt(x, bm, cm, z, dt, dtt, m_dt_bias[o], a, d_e, nw, bp)
            xp = _out1(xp, y, m_out_w[o])
            outs["mcp"].append(mcp)
            outs["ssp"].append(hT.reshape(bp, nh, M_HEAD_DIM, M_D_STATE))

            z, xbc, dt, _ = _m_in(xs, mnw, wz, wx, wdt, wdtt)
            y, mcs, hn = _m_step(state_mamba_conv[o], xbc, z, dt, state_ssm[o], m_conv_w[o], cb,
                                 m_dt_bias[o], a, d_e, nw)
            xs = _out1(xs, y.reshape(bs, d_inner), m_out_w[o])
            outs["mcs"].append(mcs)
            outs["sss"].append(hn.reshape(bs, nh, M_HEAD_DIM, M_D_STATE))
        last = layer == depth - 1
        w2 = (n2, f2g, f2u, f2d, fw, layer)
        xp = _ffn(xp, *w2, final=last)
        xs = _ffn(xs, *w2, final=last)
    st = lambda key: jnp.stack(outs[key])
    return (xp.reshape(bp, seq, d), xs.reshape(bs, 1, d),
            st("kp"), st("vp"), st("cap"), st("mcp"), st("ssp"),
            st("ks"), st("vs"), st("cas"), st("mcs"), st("sss"))
```

```python
import functools

import jax
import jax.numpy as jnp
from jax import lax
from jax.experimental import pallas as pl
from jax.experimental.pallas import tpu as pltpu

F32 = jnp.float32
BF16 = jnp.bfloat16
EPS = 1e-6

LANES = 128
SUBLANES = 8
VMEM_LIMIT_BYTES = 56 * 1024 * 1024

SB_HEADS = 8
SB_HEAD_DIM = 64
CONV_W = 31
M_HEAD_DIM = 64
M_GROUPS = 4
M_D_STATE = 128
M_CONV_W = 4
SSD_CHUNK = 128


def _cparams(*sem):
    return pltpu.CompilerParams(dimension_semantics=sem, vmem_limit_bytes=VMEM_LIMIT_BYTES)


def _resident(shape):
    nd = len(shape)
    return pl.BlockSpec(shape, lambda *_: (0,) * nd, pipeline_mode=pl.Buffered(1))


def _resident_layer(shape, layer):
    nd = len(shape)
    return pl.BlockSpec((None,) + tuple(shape[1:]), lambda *_: (layer,) + (0,) * (nd - 1),
                        pipeline_mode=pl.Buffered(1))


def _rms_bf16(x, w):
    y = x * lax.rsqrt(jnp.mean(x * x, axis=-1, keepdims=True) + EPS)
    return (y * w).astype(BF16)


def _dot(a, b):
    return jnp.dot(a, b, preferred_element_type=F32)


def _dot_nt(a, b):
    return lax.dot_general(a, b, (((1,), (1,)), ((), ())), preferred_element_type=F32)


def _split2(v):
    hi = v.astype(BF16)
    lo = (v - hi.astype(F32)).astype(BF16)
    return hi, lo


def _split3(v):
    hi = v.astype(BF16)
    r = v - hi.astype(F32)
    mid = r.astype(BF16)
    lo = (r - mid.astype(F32)).astype(BF16)
    return hi, mid, lo


def _dot_exact_rhs(parts, m):
    acc = _dot(parts[0], m)
    for p in parts[1:]:
        acc = acc + _dot(p, m)
    return acc


def _softplus(z):
    return jnp.maximum(z, 0.0) + jnp.log1p(jnp.exp(-jnp.abs(z)))


def _silu(v):
    return v * jax.nn.sigmoid(v)


def _token_tile(t, want):
    return want if t % want == 0 else t


def _ffn_body(x_ref, nw_ref, wg_ref, wu_ref, wd_ref, fw_ref, o_ref, *, final):
    x = x_ref[...]
    h = _rms_bf16(x, nw_ref[...])
    g = _dot(h, wg_ref[...])
    u = _dot(h, wu_ref[...])
    a = (_silu(g) * u).astype(BF16)
    y = x + 0.5 * _dot(a, wd_ref[...])
    if final:
        y = y * lax.rsqrt(jnp.mean(y * y, axis=-1, keepdims=True) + EPS) * fw_ref[...]
    o_ref[...] = y


def _ffn(x, nw, wg, wu, wd, fw, layer, *, final=False):
    t, d = x.shape
    tm = _token_tile(t, 512)
    return pl.pallas_call(
        functools.partial(_ffn_body, final=final),
        grid=(t // tm,),
        in_specs=[pl.BlockSpec((tm, d), lambda i: (i, 0)),
                  _resident_layer(nw.shape, layer), _resident_layer(wg.shape, layer),
                  _resident_layer(wu.shape, layer), _resident_layer(wd.shape, layer),
                  _resident((1, d))],
        out_specs=pl.BlockSpec((tm, d), lambda i: (i, 0)),
        out_shape=jax.ShapeDtypeStruct((t, d), F32),
        compiler_params=_cparams("parallel"),
        name="ffn",
    )(x, nw, wg, wu, wd, fw)


def _ab_in_body(x_ref, nw_ref, w_ref, u_ref, qb_ref, k_ref, v_ref, kb_ref, vb_ref, *, c, hd, key_minor):
    h = _rms_bf16(x_ref[...], nw_ref[...])
    p = _dot(h, w_ref[...])
    u_ref[...] = p[:, :c] * jax.nn.sigmoid(p[:, c:2 * c])
    qb_ref[...] = (p[:, 2 * c:2 * c + hd] * (SB_HEAD_DIM ** -0.5)).astype(BF16)
    k = p[:, 2 * c + hd:2 * c + 2 * hd]
    v = p[:, 2 * c + 2 * hd:]
    k_ref[...] = k.T if key_minor else k
    v_ref[...] = v.T if key_minor else v
    kb_ref[...] = k.astype(BF16)
    vb_ref[...] = v.astype(BF16)


def _ab_in(x, nw, w, bsz=None):
    t, d = x.shape
    hd = SB_HEADS * SB_HEAD_DIM
    c = (w.shape[1] - 3 * hd) // 2
    tm = _token_tile(t, 512)
    row = lambda n: pl.BlockSpec((tm, n), lambda i: (i, 0))
    if bsz is None:
        kv_spec, kv_shape = row(hd), jax.ShapeDtypeStruct((t, hd), F32)
    else:
        assert (t // bsz) % tm == 0
        nt = t // bsz // tm
        kv_spec = pl.BlockSpec((None, hd, tm), lambda i: (i // nt, 0, i % nt))
        kv_shape = jax.ShapeDtypeStruct((bsz, hd, t // bsz), F32)
    return pl.pallas_call(
        functools.partial(_ab_in_body, c=c, hd=hd, key_minor=bsz is not None),
        grid=(t // tm,),
        in_specs=[row(d), _resident((1, d)), _resident(w.shape)],
        out_specs=[row(c), row(hd), kv_spec, kv_spec, row(hd), row(hd)],
        out_shape=[jax.ShapeDtypeStruct((t, c), F32), jax.ShapeDtypeStruct((t, hd), BF16),
                   kv_shape, kv_shape,
                   jax.ShapeDtypeStruct((t, hd), BF16), jax.ShapeDtypeStruct((t, hd), BF16)],
        compiler_params=_cparams("parallel"),
        name="ab_in",
    )(x, nw, w)


_CONV_ROWS = 32
_CONV_HALO = 32


def _layernorm_silu(cv, g, b):
    mu = jnp.mean(cv, axis=-1, keepdims=True)
    d = cv - mu
    var = jnp.mean(d * d, axis=-1, keepdims=True)
    return _silu(d * lax.rsqrt(var + EPS) * g + b)


def _conv_a_body(u_ref, w_ref, b_ref, g_ref, be_ref, a_ref, cn_ref, ext_ref, sh_ref, *, tt):
    t = pl.program_id(1)

    @pl.when(t == 0)
    def _():
        ext_ref[0:_CONV_HALO, :] = jnp.zeros((_CONV_HALO, ext_ref.shape[1]), F32)

    ext_ref[_CONV_HALO:_CONV_HALO + tt, :] = u_ref[...]
    first = _CONV_HALO - (CONV_W - 1)
    taps = [list(range(s, CONV_W, SUBLANES)) for s in range(SUBLANES)]
    for s in range(SUBLANES):
        span = tt + taps[s][-1] - s
        sh_ref[s, 0:span, :] = ext_ref[first + s:first + s + span, :]
    for r in range(tt // _CONV_ROWS):
        acc = None
        for s in range(SUBLANES):
            for j in taps[s]:
                lo = r * _CONV_ROWS + j - s
                term = w_ref[j:j + 1, :] * sh_ref[s, lo:lo + _CONV_ROWS, :]
                acc = term if acc is None else acc + term
        cv = acc + b_ref[...]
        a_ref[r * _CONV_ROWS:(r + 1) * _CONV_ROWS, :] = _layernorm_silu(cv, g_ref[...], be_ref[...]).astype(BF16)

    @pl.when(t == pl.num_programs(1) - 1)
    def _():
        cn_ref[...] = ext_ref[_CONV_HALO + tt - (CONV_W - 1):_CONV_HALO + tt, :]

    ext_ref[0:_CONV_HALO, :] = ext_ref[tt:tt + _CONV_HALO, :]


def _conv_a_prompt(u, bsz, w, b, g, be):
    t_all, c = u.shape
    t = t_all // bsz
    tt = _token_tile(t, 512)
    nt = t // tt
    return pl.pallas_call(
        functools.partial(_conv_a_body, tt=tt),
        grid=(bsz, nt),
        in_specs=[pl.BlockSpec((tt, c), lambda i, j: (i * nt + j, 0)),
                  _resident(w.shape), _resident((1, c)), _resident((1, c)), _resident((1, c))],
        out_specs=[pl.BlockSpec((tt, c), lambda i, j: (i * nt + j, 0)),
                   pl.BlockSpec((None, CONV_W - 1, c), lambda i, j: (i, 0, 0))],
        out_shape=[jax.ShapeDtypeStruct((t_all, c), BF16),
                   jax.ShapeDtypeStruct((bsz, CONV_W - 1, c), F32)],
        scratch_shapes=[pltpu.VMEM((_CONV_HALO + tt, c), F32),
                        pltpu.VMEM((SUBLANES, tt + _CONV_HALO - SUBLANES, c), F32)],
        compiler_params=_cparams("parallel", "arbitrary"),
        name="conv_a_prompt",
    )(u, w, b, g, be)


def _conv_a_sample_body(prev_ref, u_ref, w_ref, b_ref, g_ref, be_ref, a_ref, cn_ref):
    nprev = CONV_W - 1
    prev = prev_ref[...]
    u = u_ref[...]
    cv = jnp.sum(w_ref[0:nprev, :] * prev, axis=0, keepdims=True) + w_ref[nprev:CONV_W, :] * u + b_ref[...]
    a_ref[...] = _layernorm_silu(cv, g_ref[...], be_ref[...])
    cn_ref[0:nprev - 1, :] = prev_ref[1:nprev, :]
    cn_ref[nprev - 1:nprev, :] = u


def _conv_a_sample(prev, u, w, b, g, be):
    bsz, nprev, c = prev.shape
    return pl.pallas_call(
        _conv_a_sample_body,
        grid=(bsz,),
        in_specs=[pl.BlockSpec((None, nprev, c), lambda i: (i, 0, 0)),
                  pl.BlockSpec((None, 1, c), lambda i: (i, 0, 0)),
                  _resident(w.shape), _resident((1, c)), _resident((1, c)), _resident((1, c))],
        out_specs=[pl.BlockSpec((None, 1, c), lambda i: (i, 0, 0)),
                   pl.BlockSpec((None, nprev, c), lambda i: (i, 0, 0))],
        out_shape=[jax.ShapeDtypeStruct((bsz, 1, c), F32),
                   jax.ShapeDtypeStruct((bsz, nprev, c), F32)],
        compiler_params=_cparams("parallel"),
        name="conv_a_sample",
    )(prev, u.reshape(bsz, 1, c), w, b, g, be)


LOG2E = 1.4426950408889634


def _suffix_matrix(tk, with_total):
    cols = tk + LANES if with_total else tk
    j = lax.broadcasted_iota(jnp.int32, (tk, cols), 0)
    s = lax.broadcasted_iota(jnp.int32, (tk, cols), 1)
    return jnp.where((j > s) | (s >= tk), 1.0, 0.0).astype(BF16)


def _neg_abs(x):
    return lax.bitcast_convert_type(lax.bitcast_convert_type(x, jnp.uint32) | jnp.uint32(0x80000000), F32)


_SB_STRIP = 32


def _sb_block(z, bias2, v_masked, sufm, carry, diagonal):
    tq, tk = z.shape
    strips = [slice(r * _SB_STRIP, (r + 1) * _SB_STRIP) for r in range(tq // _SB_STRIP)]

    def admitted(r):
        rowi = lax.broadcasted_iota(jnp.int32, (_SB_STRIP, tk), 0) + r * _SB_STRIP
        coli = lax.broadcasted_iota(jnp.int32, (_SB_STRIP, tk), 1)
        return coli < rowi

    sp_parts, arg_parts, first = [], [], []
    for r, rows in enumerate(strips):
        zs = z[rows] * LOG2E + bias2
        sp = jnp.maximum(zs, 0.0) + jnp.log2(1.0 + jnp.exp2(_neg_abs(zs)))
        arg_parts.append(zs - sp)
        if diagonal:
            sp = jnp.where(admitted(r), sp, 0.0)
        first.append(sp[:, 0:1])
        sp_parts.append(sp.astype(BF16))
    sums = _dot(jnp.concatenate(sp_parts, axis=0), sufm)
    w_parts = []
    for r, rows in enumerate(strips):
        cr = carry[rows]
        after = jnp.concatenate([sums[rows, c * LANES:(c + 1) * LANES] + cr for c in range(tk // LANES)], axis=1)
        wgt = jnp.exp2(arg_parts[r] - after)
        if diagonal:
            wgt = jnp.where(admitted(r), wgt, 0.0)
        w_parts.append(wgt.astype(BF16))
    out = _dot(jnp.concatenate(w_parts, axis=0), v_masked)
    total = sums[:, 0:1] + jnp.concatenate(first, axis=0)
    return out, carry + jnp.broadcast_to(total, carry.shape)


def _sb_prompt_body(bias_ref, q_ref, k_ref, v_ref, o_ref, qm_ref, acc_ref, carry_ref, *, tq):
    i = pl.program_id(1)
    tk = tq
    sufm = _suffix_matrix(tk, False)
    lane = lax.broadcasted_iota(jnp.int32, (1, LANES), 1)
    sels = [lane < SB_HEAD_DIM, lane >= SB_HEAD_DIM]
    for head in range(SB_HEADS):
        qp = q_ref[:, (head // 2) * LANES:(head // 2 + 1) * LANES]
        qm_ref[head] = jnp.where(sels[head % 2], qp, jnp.zeros_like(qp))
    acc_ref[...] = jnp.zeros_like(acc_ref)
    carry_ref[...] = jnp.zeros_like(carry_ref)

    def block(kj, diagonal):
        start = pl.multiple_of(kj * tk, tk)
        kbs = [k_ref[pl.ds(start, tk), hp * LANES:(hp + 1) * LANES] for hp in range(SB_HEADS // 2)]
        vbs = [v_ref[pl.ds(start, tk), hp * LANES:(hp + 1) * LANES] for hp in range(SB_HEADS // 2)]
        carries = [carry_ref[head] for head in range(SB_HEADS)]
        zs = [_dot_nt(qm_ref[head], kbs[head // 2]) for head in range(SB_HEADS)]
        outs = []
        for head in range(SB_HEADS):
            vm = jnp.where(sels[head % 2], vbs[head // 2], jnp.zeros_like(vbs[head // 2]))
            o, carries[head] = _sb_block(zs[head], bias_ref[head] * LOG2E, vm, sufm, carries[head], diagonal)
            outs.append(o)
        for head in range(SB_HEADS):
            carry_ref[head] = carries[head]
        for hp in range(SB_HEADS // 2):
            acc_ref[:, hp * LANES:(hp + 1) * LANES] += outs[2 * hp] + outs[2 * hp + 1]

    block(i, True)

    def step(jj, _):
        block(i - jj, False)
        return 0

    lax.fori_loop(1, i + 1, step, 0)
    o_ref[...] = acc_ref[...].astype(o_ref.dtype)


def _sb_attn_prompt(qb, kb, vb, bias, bsz):
    t_all, hd = qb.shape
    t = t_all // bsz
    tq = _token_tile(t, 256)
    nq = t // tq
    return pl.pallas_call(
        functools.partial(_sb_prompt_body, tq=tq),
        grid=(bsz, nq),
        in_specs=[pl.BlockSpec(memory_space=pltpu.SMEM),
                  pl.BlockSpec((tq, hd), lambda b, i: (b * nq + i, 0)),
                  pl.BlockSpec((t, hd), lambda b, i: (b, 0)),
                  pl.BlockSpec((t, hd), lambda b, i: (b, 0))],
        out_specs=pl.BlockSpec((tq, hd), lambda b, i: (b * nq + i, 0)),
        out_shape=jax.ShapeDtypeStruct((t_all, hd), BF16),
        scratch_shapes=[pltpu.VMEM((SB_HEADS, tq, LANES), BF16), pltpu.VMEM((tq, hd), F32),
                        pltpu.VMEM((SB_HEADS, tq, LANES), F32)],
        compiler_params=_cparams("parallel", "arbitrary"),
        name="sb_attn_prompt",
    )(bias, qb, kb, vb)


_DEC_PAGES = 16


def _sb_decode_body(pt_ref, bias_ref, q_ref, *refs, page, npg):
    k_refs, v_refs = refs[:npg], refs[npg:2 * npg]
    o_ref, acc_ref, carry_ref = refs[2 * npg:]
    p = pl.program_id(1)
    hd = SB_HEADS * SB_HEAD_DIM

    @pl.when(p == 0)
    def _():
        acc_ref[...] = jnp.zeros_like(acc_ref)
        carry_ref[...] = jnp.zeros_like(carry_ref)

    own = (lax.broadcasted_iota(jnp.int32, (SB_HEADS, hd), 1) // SB_HEAD_DIM
           == lax.broadcasted_iota(jnp.int32, (SB_HEADS, hd), 0))
    q8 = jnp.where(own, jnp.broadcast_to(q_ref[...], (SB_HEADS, hd)), 0.0).astype(BF16)
    sufm = _suffix_matrix(page, True)
    args, pieces = [], []
    for g in range(npg):
        kt = k_refs[g][...].reshape(hd, page).astype(BF16)
        z = _dot(q8, kt) + bias_ref[...]
        sp = _softplus(z)
        args.append(z - sp)
        pieces.append(sp)
    sums = _dot_exact_rhs(_split2(jnp.concatenate(pieces, axis=0)), sufm)
    acc = acc_ref[...]
    run = carry_ref[...]
    for g in range(npg):
        rows = slice(g * SB_HEADS, (g + 1) * SB_HEADS)
        wgt = jnp.exp(args[g] - (sums[rows, :page] + run))
        run = run + sums[rows, page:]
        vt = v_refs[g][...].reshape(hd, page).astype(BF16)
        acc = acc + _dot_nt(wgt.astype(BF16), vt)
    acc_ref[...] = acc
    carry_ref[...] = run

    @pl.when(p == pl.num_programs(1) - 1)
    def _():
        o_ref[...] = jnp.sum(jnp.where(own, acc, 0.0), axis=0, keepdims=True)


def _sb_attn_decode(q, cache_kt, cache_vt, layer, page_table, bias):
    bsz, hd = q.shape
    n_pages = page_table.shape[1]
    page = cache_kt.shape[4]
    assert page == LANES
    npg = _DEC_PAGES if n_pages % _DEC_PAGES == 0 else 1

    def page_spec(g):
        return pl.BlockSpec((None, None, SB_HEADS, SB_HEAD_DIM, page),
                            lambda b, p, pt: (layer, pt[b * n_pages + n_pages - 1 - (p * npg + g)], 0, 0, 0))

    per_b = pl.BlockSpec((None, 1, hd), lambda b, p, pt: (b, 0, 0))
    grid_spec = pltpu.PrefetchScalarGridSpec(
        num_scalar_prefetch=1,
        grid=(bsz, n_pages // npg),
        in_specs=[pl.BlockSpec((SB_HEADS, 1), lambda b, p, pt: (0, 0)), per_b]
                 + [page_spec(g) for g in range(npg)] * 2,
        out_specs=per_b,
        scratch_shapes=[pltpu.VMEM((SB_HEADS, hd), F32), pltpu.VMEM((SB_HEADS, LANES), F32)],
    )
    return pl.pallas_call(
        functools.partial(_sb_decode_body, page=page, npg=npg),
        grid_spec=grid_spec,
        out_shape=jax.ShapeDtypeStruct((bsz, 1, hd), F32),
        compiler_params=_cparams("parallel", "arbitrary"),
        name="sb_attn_decode",
    )(page_table.reshape(-1), bias.reshape(SB_HEADS, 1), q.reshape(bsz, 1, hd),
      *([cache_kt] * npg), *([cache_vt] * npg))


def _out2_body(x_ref, a_ref, o_ref, wa_ref, wo_ref, y_ref):
    y_ref[...] = (x_ref[...] + _dot(a_ref[...].astype(BF16), wa_ref[...])
                  + _dot(o_ref[...].astype(BF16), wo_ref[...]))


def _out2(x, a, o, wa, wo):
    t, d = x.shape
    tm = _token_tile(t, 512)
    row = lambda n: pl.BlockSpec((tm, n), lambda i: (i, 0))
    return pl.pallas_call(
        _out2_body,
        grid=(t // tm,),
        in_specs=[row(d), row(a.shape[1]), row(o.shape[1]), _resident(wa.shape), _resident(wo.shape)],
        out_specs=row(d),
        out_shape=jax.ShapeDtypeStruct((t, d), F32),
        compiler_params=_cparams("parallel"),
        name="ab_out",
    )(x, a, o, wa, wo)


def _out1_body(x_ref, y_ref, w_ref, o_ref):
    o_ref[...] = x_ref[...] + _dot(y_ref[...].astype(BF16), w_ref[...])


def _out1(x, y, w):
    t, d = x.shape
    tm = _token_tile(t, 512)
    row = lambda n: pl.BlockSpec((tm, n), lambda i: (i, 0))
    return pl.pallas_call(
        _out1_body,
        grid=(t // tm,),
        in_specs=[row(d), row(y.shape[1]), _resident(w.shape)],
        out_specs=row(d),
        out_shape=jax.ShapeDtypeStruct((t, d), F32),
        compiler_params=_cparams("parallel"),
        name="m_out",
    )(x, y, w)


def _m_in_body(x_ref, nw_ref, wz_ref, wx_ref, wdt_ref, wdtt_ref, z_ref, xbc_ref, dt_ref, dtt_ref):
    h = _rms_bf16(x_ref[...], nw_ref[...])
    z_ref[...] = _dot(h, wz_ref[...])
    xbc_ref[...] = _dot(h, wx_ref[...])
    dt_ref[...] = _dot(h, wdt_ref[...])
    dtt_ref[...] = _dot_nt(wdtt_ref[...], h)


def _m_in(x, nw, wz, wx, wdt, wdtt):
    t, d = x.shape
    tm = _token_tile(t, 512)
    nh = wdt.shape[1]
    row = lambda n: pl.BlockSpec((tm, n), lambda i: (i, 0))
    return pl.pallas_call(
        _m_in_body,
        grid=(t // tm,),
        in_specs=[row(d), _resident((1, d)), _resident(wz.shape), _resident(wx.shape),
                  _resident(wdt.shape), _resident(wdtt.shape)],
        out_specs=[row(wz.shape[1]), row(wx.shape[1]), row(nh), pl.BlockSpec((nh, tm), lambda i: (0, i))],
        out_shape=[jax.ShapeDtypeStruct((t, wz.shape[1]), F32), jax.ShapeDtypeStruct((t, wx.shape[1]), F32),
                   jax.ShapeDtypeStruct((t, nh), F32), jax.ShapeDtypeStruct((nh, t), F32)],
        compiler_params=_cparams("parallel"),
        name="m_in",
    )(x, nw, wz, wx, wdt, wdtt)


_MCONV_LANES = 512


def _m_conv_body(xbc_ref, w_ref, b_ref, x_ref, bm_ref, cm_ref, cn_ref, ext_ref, *, tt, d_inner, gn):
    t = pl.program_id(1)
    halo = SUBLANES
    nprev = M_CONV_W - 1

    @pl.when(t == 0)
    def _():
        ext_ref[0:halo, :] = jnp.zeros((halo, ext_ref.shape[1]), F32)

    ext_ref[halo:halo + tt, :] = xbc_ref[...]
    first = halo - nprev
    cdim = ext_ref.shape[1]
    for lc in range(cdim // _MCONV_LANES):
        ls = slice(lc * _MCONV_LANES, (lc + 1) * _MCONV_LANES)
        for r in range(tt // _CONV_ROWS):
            base = r * _CONV_ROWS + first
            acc = w_ref[0:1, ls] * ext_ref[base:base + _CONV_ROWS, ls]
            for j in range(1, M_CONV_W):
                acc = acc + w_ref[j:j + 1, ls] * ext_ref[base + j:base + j + _CONV_ROWS, ls]
            val = _silu(acc + b_ref[:, ls])
            rows = slice(r * _CONV_ROWS, (r + 1) * _CONV_ROWS)
            lo = lc * _MCONV_LANES
            if lo < d_inner:
                x_ref[rows, lo:lo + _MCONV_LANES] = val
            elif lo < d_inner + gn:
                bm_ref[rows, lo - d_inner:lo - d_inner + _MCONV_LANES] = val
            else:
                cm_ref[rows, lo - d_inner - gn:lo - d_inner - gn + _MCONV_LANES] = val.astype(BF16)

    @pl.when(t == pl.num_programs(1) - 1)
    def _():
        cn_ref[...] = ext_ref[halo + tt - nprev:halo + tt, :]

    ext_ref[0:halo, :] = ext_ref[tt:tt + halo, :]


def _m_conv_prompt(xbc, bsz, w, b, d_inner):
    t_all, cdim = xbc.shape
    gn = M_GROUPS * M_D_STATE
    assert gn == _MCONV_LANES and d_inner % _MCONV_LANES == 0
    t = t_all // bsz
    tt = _token_tile(t, 256)
    nt = t // tt
    row = lambda n: pl.BlockSpec((tt, n), lambda i, j: (i * nt + j, 0))
    return pl.pallas_call(
        functools.partial(_m_conv_body, tt=tt, d_inner=d_inner, gn=gn),
        grid=(bsz, nt),
        in_specs=[row(cdim), _resident(w.shape), _resident((1, cdim))],
        out_specs=[row(d_inner), row(gn), row(gn),
                   pl.BlockSpec((None, M_CONV_W - 1, cdim), lambda i, j: (i, 0, 0))],
        out_shape=[jax.ShapeDtypeStruct((t_all, d_inner), F32), jax.ShapeDtypeStruct((t_all, gn), F32),
                   jax.ShapeDtypeStruct((t_all, gn), BF16),
                   jax.ShapeDtypeStruct((bsz, M_CONV_W - 1, cdim), F32)],
        scratch_shapes=[pltpu.VMEM((SUBLANES + tt, cdim), F32)],
        compiler_params=_cparams("parallel", "arbitrary"),
        name="m_conv_prompt",
    )(xbc, w, b)


_EXPAND_PIECES = 3


def _head_expand_matrix(nh, width):
    h = lax.broadcasted_iota(jnp.int32, (_EXPAND_PIECES * nh, nh * width), 0) % nh
    c = lax.broadcasted_iota(jnp.int32, (_EXPAND_PIECES * nh, nh * width), 1)
    return jnp.where(c // width == h, 1.0, 0.0).astype(BF16)


def _expand_heads(v, expand):
    return _dot(jnp.concatenate(_split3(v), axis=1), expand)


def _gate_norm(y, z, nw, d_inner):
    y = y * _silu(z)
    gw = d_inner // M_GROUPS
    parts = []
    for g in range(M_GROUPS):
        yg = y[:, g * gw:(g + 1) * gw]
        parts.append(yg * lax.rsqrt(jnp.mean(yg * yg, axis=-1, keepdims=True) + EPS))
    return jnp.concatenate(parts, axis=1) * nw


def _ssd_body(x_ref, bm_ref, cm_ref, z_ref, dt_ref, dtt_ref, dtb_ref, dtbt_ref, a_ref, at_ref, de_ref, nw_ref,
              y_ref, hT_ref, h_ref, yd_ref, *, nh, d_inner):
    c = pl.program_id(1)
    L = SSD_CHUNK
    hpg = nh // M_GROUPS
    gw = hpg * M_HEAD_DIM

    @pl.when(c == 0)
    def _():
        h_ref[...] = jnp.zeros_like(h_ref)

    dt = _softplus(dt_ref[...] + dtb_ref[...])
    dtt = _softplus(dtt_ref[...] + dtbt_ref[...])
    row = lax.broadcasted_iota(jnp.int32, (L, L), 0)
    col = lax.broadcasted_iota(jnp.int32, (L, L), 1)
    causal = col <= row
    tri = jnp.where(causal, 1.0, 0.0).astype(BF16)
    trit = jnp.where(row <= col, 1.0, 0.0).astype(BF16)
    dta_hi, dta_lo = _split2(dt * a_ref[...])
    acs = _dot(tri, dta_hi) + _dot(tri, dta_lo)
    acst = _dot_exact_rhs(_split2(dtt * at_ref[...]), trit)
    acs_last = acs[L - 1:L, :]
    expand = _head_expand_matrix(nh, M_HEAD_DIM)
    dt_e = _expand_heads(dt, expand)
    eacs_e = _expand_heads(jnp.exp(acs), expand)
    eend_e = _expand_heads(jnp.exp(acs_last - acs), expand)
    etot_e = _expand_heads(jnp.broadcast_to(jnp.exp(acs_last), (SUBLANES, nh)), expand)[0:1, :]

    x = x_ref[...]
    dtx = x * dt_e
    dtx_b = dtx.astype(BF16)
    dtxe_b = (dtx * eend_e).astype(BF16)
    lane = lax.broadcasted_iota(jnp.int32, (1, LANES), 1)
    assert 2 * M_HEAD_DIM == LANES and hpg % 2 == 0
    for g in range(M_GROUPS):
        ns = slice(g * M_D_STATE, (g + 1) * M_D_STATE)
        gs = slice(g * gw, (g + 1) * gw)
        bg = bm_ref[:, ns]
        cg = cm_ref[:, ns]
        cb = _dot_nt(cg, bg.astype(BF16))
        hg = h_ref[:, gs]
        yd_ref[:, gs] = _dot(cg, hg.astype(BF16)) * eacs_e[:, gs]
        h_ref[:, gs] = hg * etot_e[:, gs] + _dot(bg.T.astype(BF16), dtxe_b[:, gs])
        for hp in range(hpg // 2):
            pair = g * hpg + 2 * hp
            ps = slice(pair * M_HEAD_DIM, (pair + 2) * M_HEAD_DIM)
            ms = []
            for head in (pair, pair + 1):
                dec = jnp.exp(jnp.where(causal, acs[:, head:head + 1] - acst[head:head + 1, :], -jnp.inf))
                ms.append((cb * dec).astype(BF16))
            dp = dtx_b[:, ps]
            rhs = jnp.concatenate([jnp.where(lane < M_HEAD_DIM, dp, jnp.zeros_like(dp)),
                                   jnp.where(lane >= M_HEAD_DIM, dp, jnp.zeros_like(dp))], axis=0)
            yd_ref[:, ps] += _dot(jnp.concatenate(ms, axis=1), rhs)

    y = yd_ref[...] + de_ref[...] * x
    y_ref[...] = _gate_norm(y, z_ref[...], nw_ref[...], d_inner).astype(BF16)

    @pl.when(c == pl.num_programs(1) - 1)
    def _():
        hT_ref[...] = h_ref[...].T


def _ssd_prompt(x, bm, cm, z, dt, dtt, dtb, a, d_e, nw, bsz):
    t_all, d_inner = x.shape
    nh = dt.shape[1]
    gn = bm.shape[1]
    t = t_all // bsz
    L = SSD_CHUNK
    assert t % L == 0
    nc = t // L
    row = lambda n: pl.BlockSpec((L, n), lambda b, c: (b * nc + c, 0))
    return pl.pallas_call(
        functools.partial(_ssd_body, nh=nh, d_inner=d_inner),
        grid=(bsz, nc),
        in_specs=[row(d_inner), row(gn), row(gn), row(d_inner), row(nh),
                  pl.BlockSpec((nh, L), lambda b, c: (0, b * nc + c)),
                  _resident((1, nh)), _resident((nh, 1)), _resident((1, nh)), _resident((nh, 1)),
                  _resident((1, d_inner)), _resident((1, d_inner))],
        out_specs=[row(d_inner), pl.BlockSpec((None, d_inner, M_D_STATE), lambda b, c: (b, 0, 0))],
        out_shape=[jax.ShapeDtypeStruct((t_all, d_inner), BF16),
                   jax.ShapeDtypeStruct((bsz, d_inner, M_D_STATE), F32)],
        scratch_shapes=[pltpu.VMEM((M_D_STATE, d_inner), F32), pltpu.VMEM((L, d_inner), F32)],
        compiler_params=_cparams("parallel", "arbitrary"),
        name="ssd_prompt",
    )(x, bm, cm, z, dt, dtt, dtb.reshape(1, nh), dtb.reshape(nh, 1), a.reshape(1, nh), a.reshape(nh, 1), d_e, nw)


def _m_step_body(cprev_ref, xbc_ref, z_ref, dt_ref, st_ref, w_ref, b_ref, dtb_ref, a_ref, de_ref, nw_ref,
                 y_ref, cn_ref, hn_ref, *, nh, d_inner):
    nprev = M_CONV_W - 1
    gn = M_GROUPS * M_D_STATE
    gw = d_inner // M_GROUPS
    new = xbc_ref[...]
    acc = w_ref[nprev:M_CONV_W, :] * new + b_ref[...]
    for j in range(nprev):
        acc = acc + w_ref[j:j + 1, :] * cprev_ref[j:j + 1, :]
    cn_ref[0:nprev - 1, :] = cprev_ref[1:nprev, :]
    cn_ref[nprev - 1:nprev, :] = new
    xbc = _silu(acc)
    x = xbc[:, :d_inner]

    dt = _softplus(dt_ref[...] + dtb_ref[...])
    expand = _head_expand_matrix(nh, M_HEAD_DIM)
    rows8 = lambda v: jnp.broadcast_to(v, (SUBLANES, nh))
    dt_e = _expand_heads(rows8(dt), expand)[0:1, :]
    da_e = _expand_heads(rows8(jnp.exp(dt * a_ref[...])), expand)[0:1, :]
    dtx = x * dt_e

    rowi = lax.broadcasted_iota(jnp.int32, (M_D_STATE, M_D_STATE), 0)
    b_mat = jnp.zeros((M_D_STATE, M_D_STATE), F32)
    c_mat = jnp.zeros((M_D_STATE, M_D_STATE), F32)
    for g in range(M_GROUPS):
        b_row = xbc[:, d_inner + g * M_D_STATE:d_inner + (g + 1) * M_D_STATE]
        c_row = xbc[:, d_inner + gn + g * M_D_STATE:d_inner + gn + (g + 1) * M_D_STATE]
        b_mat = jnp.where(rowi == g, b_row, b_mat)
        c_mat = jnp.where(rowi == g, c_row, c_mat)
    b_cols = b_mat.T
    c_cols = c_mat.T

    ht = st_ref[...].T
    hn_parts, y_parts = [], []
    for g in range(M_GROUPS):
        gs = slice(g * gw, (g + 1) * gw)
        hn = ht[:, gs] * da_e[:, gs] + b_cols[:, g:g + 1] * dtx[:, gs]
        hn_parts.append(hn)
        y_parts.append(jnp.sum(c_cols[:, g:g + 1] * hn, axis=0, keepdims=True))
    hn_ref[...] = jnp.concatenate(hn_parts, axis=1).T
    y = jnp.concatenate(y_parts, axis=1) + de_ref[...] * x
    y_ref[...] = _gate_norm(y, z_ref[...], nw_ref[...], d_inner)


def _m_step(conv_prev, xbc, z, dt, state, w, b, dtb, a, d_e, nw):
    bsz, nprev, cdim = conv_prev.shape
    d_inner = z.shape[1]
    nh = dt.shape[1]
    per_b = lambda r, n: pl.BlockSpec((None, r, n), lambda i: (i, 0, 0))
    return pl.pallas_call(
        functools.partial(_m_step_body, nh=nh, d_inner=d_inner),
        grid=(bsz,),
        in_specs=[per_b(nprev, cdim), per_b(1, cdim), per_b(1, d_inner), per_b(1, nh), per_b(d_inner, M_D_STATE),
                  _resident(w.shape), _resident((1, cdim)), _resident((1, nh)), _resident((1, nh)),
                  _resident((1, d_inner)), _resident((1, d_inner))],
        out_specs=[per_b(1, d_inner), per_b(nprev, cdim), per_b(d_inner, M_D_STATE)],
        out_shape=[jax.ShapeDtypeStruct((bsz, 1, d_inner), F32),
                   jax.ShapeDtypeStruct((bsz, nprev, cdim), F32),
                   jax.ShapeDtypeStruct((bsz, d_inner, M_D_STATE), F32)],
        compiler_params=_cparams("parallel"),
        name="m_step",
    )(conv_prev, xbc.reshape(bsz, 1, cdim), z.reshape(bsz, 1, d_inner), dt.reshape(bsz, 1, nh),
      state.reshape(bsz, d_inner, M_D_STATE), w, b, dtb.reshape(1, nh), a.reshape(1, nh), d_e, nw)


def kernel(x_prompt, x_sample, cache_sb_k, cache_sb_v, state_conv_a, state_mamba_conv, state_ssm, page_table,
           ffn1_norm, ffn1_gate, ffn1_up, ffn1_down, mix_norm, ffn2_norm, ffn2_gate, ffn2_up, ffn2_down,
           final_norm, ab_in_proj, conv_a_w, conv_a_b, conv_a_ln_g, conv_a_ln_b, sb_beta_bias, ab_out_proj,
           m_in_proj, m_conv_w, m_conv_b, m_dt_bias, m_A_log, m_D, m_norm, m_out_proj):
    bp, seq, d = x_prompt.shape
    bs = x_sample.shape[0]
    depth = ffn1_norm.shape[0]
    hd = SB_HEADS * SB_HEAD_DIM
    conv_ch = conv_a_w.shape[2]
    nh = m_dt_bias.shape[1]
    d_inner = nh * M_HEAD_DIM
    conv_dim = m_conv_w.shape[2]

    bf = lambda w: w.astype(BF16)
    f1g, f1u, f1d = bf(ffn1_gate), bf(ffn1_up), bf(ffn1_down)
    f2g, f2u, f2d = bf(ffn2_gate), bf(ffn2_up), bf(ffn2_down)
    ab_in_w, ab_out_w = bf(ab_in_proj), bf(ab_out_proj)
    m_in_w, m_out_w = bf(m_in_proj), bf(m_out_proj)
    fw = final_norm.reshape(1, d)
    cache_kt = jnp.transpose(cache_sb_k, (0, 1, 3, 4, 2))
    cache_vt = jnp.transpose(cache_sb_v, (0, 1, 3, 4, 2))

    xp = x_prompt.reshape(bp * seq, d)
    xs = x_sample.reshape(bs, d)
    outs = {k: [] for k in ("kp", "vp", "cap", "mcp", "ssp", "ks", "vs", "cas", "mcs", "sss")}
    n1 = ffn1_norm.reshape(depth, 1, d)
    n2 = ffn2_norm.reshape(depth, 1, d)
    for layer in range(depth):
        w1 = (n1, f1g, f1u, f1d, fw, layer)
        xp = _ffn(xp, *w1)
        xs = _ffn(xs, *w1)
        mnw = mix_norm[layer].reshape(1, d)
        if layer % 2 == 0:
            e = layer // 2
            cw = (conv_a_w[e], conv_a_b[e].reshape(1, conv_ch), conv_a_ln_g[e].reshape(1, conv_ch),
                  conv_a_ln_b[e].reshape(1, conv_ch))
            wa, wo = ab_out_w[e][:conv_ch], ab_out_w[e][conv_ch:]
            u, qb, kt, vt, kb, vb = _ab_in(xp, mnw, ab_in_w[e], bsz=bp)
            a_out, cvp = _conv_a_prompt(u, bp, *cw)
            o = _sb_attn_prompt(qb, kb, vb, sb_beta_bias[e], bp)
            xp = _out2(xp, a_out, o, wa, wo)
            outs["kp"].append(kt)
            outs["vp"].append(vt)
            outs["cap"].append(cvp)

            u, qb, k, v, _, _ = _ab_in(xs, mnw, ab_in_w[e])
            a_out, cvs = _conv_a_sample(state_conv_a[e], u, *cw)
            o = _sb_attn_decode(qb.astype(F32), cache_kt, cache_vt, e, page_table, sb_beta_bias[e])
            xs = _out2(xs, a_out.reshape(bs, conv_ch), o.reshape(bs, hd), wa, wo)
            outs["ks"].append(k.reshape(bs, 1, SB_HEADS, SB_HEAD_DIM))
            outs["vs"].append(v.reshape(bs, 1, SB_HEADS, SB_HEAD_DIM))
            outs["cas"].append(cvs)
        else:
            o = layer // 2
            w_in = m_in_w[o]
            wz, wx, wdt = w_in[:, :d_inner], w_in[:, d_inner:d_inner + conv_dim], w_in[:, d_inner + conv_dim:]
            wdtt = wdt.T
            a = -jnp.exp(m_A_log[o])
            d_e = jnp.repeat(m_D[o], M_HEAD_DIM).reshape(1, d_inner)
            nw = m_norm[o].reshape(1, d_inner)
            cb = m_conv_b[o].reshape(1, conv_dim)

            z, xbc, dt, dtt = _m_in(xp, mnw, wz, wx, wdt, wdtt)
            x, bm, cm, mcp = _m_conv_prompt(xbc, bp, m_conv_w[o], cb, d_inner)
            y, hT = _ssd_prompt(x, bm, cm, z, dt, dtt, m_dt_bias[o], a, d_e, nw, bp)
            xp = _out1(xp, y, m_out_w[o])
            outs["mcp"].append(mcp)
            outs["ssp"].append(hT.reshape(bp, nh, M_HEAD_DIM, M_D_STATE))

            z, xbc, dt, _ = _m_in(xs, mnw, wz, wx, wdt, wdtt)
            y, mcs, hn = _m_step(state_mamba_conv[o], xbc, z, dt, state_ssm[o], m_conv_w[o], cb,
                                 m_dt_bias[o], a, d_e, nw)
            xs = _out1(xs, y.reshape(bs, d_inner), m_out_w[o])
            outs["mcs"].append(mcs)
            outs["sss"].append(hn.reshape(bs, nh, M_HEAD_DIM, M_D_STATE))
        last = layer == depth - 1
        w2 = (n2, f2g, f2u, f2d, fw, layer)
        xp = _ffn(xp, *w2, final=last)
        xs = _ffn(xs, *w2, final=last)
    st = lambda key: jnp.stack(outs[key])

    def token_major(a):
        return jnp.transpose(a.reshape(a.shape[0], bp, SB_HEADS, SB_HEAD_DIM, seq), (0, 1, 4, 2, 3))

    return (xp.reshape(bp, seq, d), xs.reshape(bs, 1, d),
            token_major(st("kp")), token_major(st("vp")), st("cap"), st("mcp"), st("ssp"),
            st("ks"), st("vs"), st("cas"), st("mcs"), st("sss"))
```

```python
import functools

import jax
import jax.numpy as jnp
from jax import lax
from jax.experimental import pallas as pl
from jax.experimental.pallas import tpu as pltpu

F32 = jnp.float32
BF16 = jnp.bfloat16
EPS = 1e-6

LANES = 128
SUBLANES = 8
VMEM_LIMIT_BYTES = 56 * 1024 * 1024

SB_HEADS = 8
SB_HEAD_DIM = 64
CONV_W = 31
M_HEAD_DIM = 64
M_GROUPS = 4
M_D_STATE = 128
M_CONV_W = 4
SSD_CHUNK = 128


def _cparams(*sem):
    return pltpu.CompilerParams(dimension_semantics=sem, vmem_limit_bytes=VMEM_LIMIT_BYTES)


def _resident(shape):
    nd = len(shape)
    return pl.BlockSpec(shape, lambda *_: (0,) * nd, pipeline_mode=pl.Buffered(1))


def _resident_layer(shape, layer):
    nd = len(shape)
    return pl.BlockSpec((None,) + tuple(shape[1:]), lambda *_: (layer,) + (0,) * (nd - 1),
                        pipeline_mode=pl.Buffered(1))


def _rms_bf16(x, w):
    y = x * lax.rsqrt(jnp.mean(x * x, axis=-1, keepdims=True) + EPS)
    return (y * w).astype(BF16)


def _dot(a, b):
    return jnp.dot(a, b, preferred_element_type=F32)


def _dot_nt(a, b):
    return lax.dot_general(a, b, (((1,), (1,)), ((), ())), preferred_element_type=F32)


def _split2(v):
    hi = v.astype(BF16)
    lo = (v - hi.astype(F32)).astype(BF16)
    return hi, lo


def _split3(v):
    hi = v.astype(BF16)
    r = v - hi.astype(F32)
    mid = r.astype(BF16)
    lo = (r - mid.astype(F32)).astype(BF16)
    return hi, mid, lo


def _dot_exact_rhs(parts, m):
    acc = _dot(parts[0], m)
    for p in parts[1:]:
        acc = acc + _dot(p, m)
    return acc


def _softplus(z):
    return jnp.maximum(z, 0.0) + jnp.log1p(jnp.exp(-jnp.abs(z)))


def _silu(v):
    return v * jax.nn.sigmoid(v)


def _token_tile(t, want):
    return want if t % want == 0 else t


def _ffn_body(x_ref, nw_ref, wg_ref, wu_ref, wd_ref, fw_ref, o_ref, *, final):
    x = x_ref[...]
    h = _rms_bf16(x, nw_ref[...])
    g = _dot(h, wg_ref[...])
    u = _dot(h, wu_ref[...])
    a = (_silu(g) * u).astype(BF16)
    y = x + 0.5 * _dot(a, wd_ref[...])
    if final:
        y = y * lax.rsqrt(jnp.mean(y * y, axis=-1, keepdims=True) + EPS) * fw_ref[...]
    o_ref[...] = y


def _ffn(x, nw, wg, wu, wd, fw, layer, *, final=False):
    t, d = x.shape
    tm = _token_tile(t, 512)
    return pl.pallas_call(
        functools.partial(_ffn_body, final=final),
        grid=(t // tm,),
        in_specs=[pl.BlockSpec((tm, d), lambda i: (i, 0)),
                  _resident_layer(nw.shape, layer), _resident_layer(wg.shape, layer),
                  _resident_layer(wu.shape, layer), _resident_layer(wd.shape, layer),
                  _resident((1, d))],
        out_specs=pl.BlockSpec((tm, d), lambda i: (i, 0)),
        out_shape=jax.ShapeDtypeStruct((t, d), F32),
        compiler_params=_cparams("parallel"),
        name="ffn",
    )(x, nw, wg, wu, wd, fw)


def _ab_in_body(x_ref, nw_ref, w_ref, u_ref, qb_ref, k_ref, v_ref, kb_ref, vb_ref, *, c, hd, key_minor):
    h = _rms_bf16(x_ref[...], nw_ref[...])
    p = _dot(h, w_ref[...])
    u_ref[...] = p[:, :c] * jax.nn.sigmoid(p[:, c:2 * c])
    qb_ref[...] = (p[:, 2 * c:2 * c + hd] * (SB_HEAD_DIM ** -0.5)).astype(BF16)
    k = p[:, 2 * c + hd:2 * c + 2 * hd]
    v = p[:, 2 * c + 2 * hd:]
    k_ref[...] = k.T if key_minor else k
    v_ref[...] = v.T if key_minor else v
    kb_ref[...] = k.astype(BF16)
    vb_ref[...] = v.astype(BF16)


def _ab_in(x, nw, w, bsz=None):
    t, d = x.shape
    hd = SB_HEADS * SB_HEAD_DIM
    c = (w.shape[1] - 3 * hd) // 2
    tm = _token_tile(t, 512)
    row = lambda n: pl.BlockSpec((tm, n), lambda i: (i, 0))
    if bsz is None:
        kv_spec, kv_shape = row(hd), jax.ShapeDtypeStruct((t, hd), F32)
    else:
        assert (t // bsz) % tm == 0
        nt = t // bsz // tm
        kv_spec = pl.BlockSpec((None, hd, tm), lambda i: (i // nt, 0, i % nt))
        kv_shape = jax.ShapeDtypeStruct((bsz, hd, t // bsz), F32)
    return pl.pallas_call(
        functools.partial(_ab_in_body, c=c, hd=hd, key_minor=bsz is not None),
        grid=(t // tm,),
        in_specs=[row(d), _resident((1, d)), _resident(w.shape)],
        out_specs=[row(c), row(hd), kv_spec, kv_spec, row(hd), row(hd)],
        out_shape=[jax.ShapeDtypeStruct((t, c), F32), jax.ShapeDtypeStruct((t, hd), BF16),
                   kv_shape, kv_shape,
                   jax.ShapeDtypeStruct((t, hd), BF16), jax.ShapeDtypeStruct((t, hd), BF16)],
        compiler_params=_cparams("parallel"),
        name="ab_in",
    )(x, nw, w)


_CONV_ROWS = 32
_CONV_HALO = 32


def _layernorm_silu(cv, g, b):
    mu = jnp.mean(cv, axis=-1, keepdims=True)
    d = cv - mu
    var = jnp.mean(d * d, axis=-1, keepdims=True)
    return _silu(d * lax.rsqrt(var + EPS) * g + b)


def _conv_a_body(u_ref, w_ref, b_ref, g_ref, be_ref, a_ref, cn_ref, ext_ref, sh_ref, *, tt):
    t = pl.program_id(1)

    @pl.when(t == 0)
    def _():
        ext_ref[0:_CONV_HALO, :] = jnp.zeros((_CONV_HALO, ext_ref.shape[1]), F32)

    ext_ref[_CONV_HALO:_CONV_HALO + tt, :] = u_ref[...]
    first = _CONV_HALO - (CONV_W - 1)
    taps = [list(range(s, CONV_W, SUBLANES)) for s in range(SUBLANES)]
    rows = _CONV_HALO + tt
    for s in range(SUBLANES):
        span = tt + taps[s][-1] - s
        sh_ref[s, 0:span, :] = pltpu.roll(ext_ref[...], rows - (first + s), axis=0)[0:span]
    for r in range(tt // _CONV_ROWS):
        acc = None
        for s in range(SUBLANES):
            for j in taps[s]:
                lo = r * _CONV_ROWS + j - s
                term = w_ref[j:j + 1, :] * sh_ref[s, lo:lo + _CONV_ROWS, :]
                acc = term if acc is None else acc + term
        cv = acc + b_ref[...]
        a_ref[r * _CONV_ROWS:(r + 1) * _CONV_ROWS, :] = _layernorm_silu(cv, g_ref[...], be_ref[...]).astype(BF16)

    @pl.when(t == pl.num_programs(1) - 1)
    def _():
        cn_ref[...] = ext_ref[_CONV_HALO + tt - (CONV_W - 1):_CONV_HALO + tt, :]

    ext_ref[0:_CONV_HALO, :] = ext_ref[tt:tt + _CONV_HALO, :]


def _conv_a_prompt(u, bsz, w, b, g, be):
    t_all, c = u.shape
    t = t_all // bsz
    tt = _token_tile(t, 512)
    nt = t // tt
    return pl.pallas_call(
        functools.partial(_conv_a_body, tt=tt),
        grid=(bsz, nt),
        in_specs=[pl.BlockSpec((tt, c), lambda i, j: (i * nt + j, 0)),
                  _resident(w.shape), _resident((1, c)), _resident((1, c)), _resident((1, c))],
        out_specs=[pl.BlockSpec((tt, c), lambda i, j: (i * nt + j, 0)),
                   pl.BlockSpec((None, CONV_W - 1, c), lambda i, j: (i, 0, 0))],
        out_shape=[jax.ShapeDtypeStruct((t_all, c), BF16),
                   jax.ShapeDtypeStruct((bsz, CONV_W - 1, c), F32)],
        scratch_shapes=[pltpu.VMEM((_CONV_HALO + tt, c), F32),
                        pltpu.VMEM((SUBLANES, tt + _CONV_HALO - SUBLANES, c), F32)],
        compiler_params=_cparams("parallel", "arbitrary"),
        name="conv_a_prompt",
    )(u, w, b, g, be)


def _conv_a_sample_body(prev_ref, u_ref, w_ref, b_ref, g_ref, be_ref, a_ref, cn_ref):
    nprev = CONV_W - 1
    prev = prev_ref[...]
    u = u_ref[...]
    cv = jnp.sum(w_ref[0:nprev, :] * prev, axis=0, keepdims=True) + w_ref[nprev:CONV_W, :] * u + b_ref[...]
    a_ref[...] = _layernorm_silu(cv, g_ref[...], be_ref[...])
    cn_ref[0:nprev - 1, :] = prev_ref[1:nprev, :]
    cn_ref[nprev - 1:nprev, :] = u


def _conv_a_sample(prev, u, w, b, g, be):
    bsz, nprev, c = prev.shape
    return pl.pallas_call(
        _conv_a_sample_body,
        grid=(bsz,),
        in_specs=[pl.BlockSpec((None, nprev, c), lambda i: (i, 0, 0)),
                  pl.BlockSpec((None, 1, c), lambda i: (i, 0, 0)),
                  _resident(w.shape), _resident((1, c)), _resident((1, c)), _resident((1, c))],
        out_specs=[pl.BlockSpec((None, 1, c), lambda i: (i, 0, 0)),
                   pl.BlockSpec((None, nprev, c), lambda i: (i, 0, 0))],
        out_shape=[jax.ShapeDtypeStruct((bsz, 1, c), F32),
                   jax.ShapeDtypeStruct((bsz, nprev, c), F32)],
        compiler_params=_cparams("parallel"),
        name="conv_a_sample",
    )(prev, u.reshape(bsz, 1, c), w, b, g, be)


def _suffix_matrix(tk, with_total):
    cols = tk + LANES if with_total else tk
    j = lax.broadcasted_iota(jnp.int32, (tk, cols), 0)
    s = lax.broadcasted_iota(jnp.int32, (tk, cols), 1)
    return jnp.where((j > s) | (s >= tk), 1.0, 0.0).astype(BF16)


def _neg_abs(x):
    return lax.bitcast_convert_type(lax.bitcast_convert_type(x, jnp.uint32) | jnp.uint32(0x80000000), F32)


_SB_STRIP = 32


def _sb_weights(z, bias, sufm, carry, diagonal):
    tq, tk = z.shape
    strips = [slice(r * _SB_STRIP, (r + 1) * _SB_STRIP) for r in range(tq // _SB_STRIP)]

    def admitted(r):
        rowi = lax.broadcasted_iota(jnp.int32, (_SB_STRIP, tk), 0) + r * _SB_STRIP
        coli = lax.broadcasted_iota(jnp.int32, (_SB_STRIP, tk), 1)
        return coli < rowi

    sp_parts, arg_parts, first = [], [], []
    for r, rows in enumerate(strips):
        zs = z[rows] + bias
        sp = jnp.maximum(zs, 0.0) + jnp.log(1.0 + jnp.exp(_neg_abs(zs)))
        arg_parts.append(zs - sp)
        if diagonal:
            sp = jnp.where(admitted(r), sp, 0.0)
        first.append(sp[:, 0:1])
        sp_parts.append(sp.astype(BF16))
    sums = _dot(jnp.concatenate(sp_parts, axis=0), sufm)
    w_parts = []
    for r, rows in enumerate(strips):
        cr = carry[rows]
        after = jnp.concatenate([sums[rows, c * LANES:(c + 1) * LANES] + cr for c in range(tk // LANES)], axis=1)
        wgt = jnp.exp(arg_parts[r] - after)
        if diagonal:
            wgt = jnp.where(admitted(r), wgt, 0.0)
        w_parts.append(wgt.astype(BF16))
    total = sums[:, 0:1] + jnp.concatenate(first, axis=0)
    return jnp.concatenate(w_parts, axis=0), carry + jnp.broadcast_to(total, carry.shape)


def _sb_prompt_body(bias_ref, q_ref, k_ref, v_ref, o_ref, qm_ref, *refs, tq):
    i = pl.program_id(1)
    tk = tq
    sufm = _suffix_matrix(tk, False)
    lane = lax.broadcasted_iota(jnp.int32, (1, LANES), 1)
    sels = [lane < SB_HEAD_DIM, lane >= SB_HEAD_DIM]
    pair_lanes = [slice(hp * LANES, (hp + 1) * LANES) for hp in range(SB_HEADS // 2)]
    for head in range(SB_HEADS):
        qp = q_ref[:, pair_lanes[head // 2]]
        qm_ref[head] = jnp.where(sels[head % 2], qp, jnp.zeros_like(qp))

    per_head = lambda k: refs[k * SB_HEADS:(k + 1) * SB_HEADS]
    z_refs, w_refs, carry_refs = per_head(0), per_head(1), per_head(2)
    acc_refs = refs[3 * SB_HEADS:]
    for r in carry_refs + acc_refs:
        r[...] = jnp.zeros_like(r)

    def logits(kj):
        start = pl.multiple_of(kj * tk, tk)
        for head in range(SB_HEADS):
            z_refs[head][...] = _dot_nt(qm_ref[head], k_ref[pl.ds(start, tk), pair_lanes[head // 2]])

    def weights(diagonal):
        for head in range(SB_HEADS):
            w_refs[head][...], carry_refs[head][...] = _sb_weights(
                z_refs[head][...], bias_ref[head], sufm, carry_refs[head][...], diagonal)

    def apply(kj):
        start = pl.multiple_of(kj * tk, tk)
        for hp in range(SB_HEADS // 2):
            vb = v_ref[pl.ds(start, tk), pair_lanes[hp]]
            acc_refs[hp][...] += (
                _dot(w_refs[2 * hp][...], jnp.where(sels[0], vb, jnp.zeros_like(vb)))
                + _dot(w_refs[2 * hp + 1][...], jnp.where(sels[1], vb, jnp.zeros_like(vb))))

    logits(i)
    weights(True)
    logits(jnp.maximum(i - 1, 0))

    def step(jj, _):
        kj = i - jj
        apply(kj + 1)
        weights(False)
        logits(jnp.maximum(kj - 1, 0))
        return 0

    lax.fori_loop(1, i + 1, step, 0)
    apply(0)
    for hp in range(SB_HEADS // 2):
        o_ref[:, pair_lanes[hp]] = acc_refs[hp][...].astype(o_ref.dtype)


def _sb_attn_prompt(qb, kb, vb, bias, bsz):
    t_all, hd = qb.shape
    t = t_all // bsz
    tq = _token_tile(t, 256)
    nq = t // tq
    return pl.pallas_call(
        functools.partial(_sb_prompt_body, tq=tq),
        grid=(bsz, nq),
        in_specs=[pl.BlockSpec(memory_space=pltpu.SMEM),
                  pl.BlockSpec((tq, hd), lambda b, i: (b * nq + i, 0)),
                  pl.BlockSpec((t, hd), lambda b, i: (b, 0)),
                  pl.BlockSpec((t, hd), lambda b, i: (b, 0))],
        out_specs=pl.BlockSpec((tq, hd), lambda b, i: (b * nq + i, 0)),
        out_shape=jax.ShapeDtypeStruct((t_all, hd), BF16),
        scratch_shapes=([pltpu.VMEM((SB_HEADS, tq, LANES), BF16)]
                        + [pltpu.VMEM((tq, tq), F32)] * SB_HEADS
                        + [pltpu.VMEM((tq, tq), BF16)] * SB_HEADS
                        + [pltpu.VMEM((tq, LANES), F32)] * SB_HEADS
                        + [pltpu.VMEM((tq, LANES), F32)] * (SB_HEADS // 2)),
        compiler_params=_cparams("parallel", "arbitrary"),
        name="sb_attn_prompt",
    )(bias, qb, kb, vb)


_DEC_PAGES = 16


def _sb_decode_body(pt_ref, bias_ref, q_ref, *refs, page, npg):
    k_refs, v_refs = refs[:npg], refs[npg:2 * npg]
    o_ref, acc_ref, carry_ref = refs[2 * npg:]
    p = pl.program_id(1)
    hd = SB_HEADS * SB_HEAD_DIM

    @pl.when(p == 0)
    def _():
        acc_ref[...] = jnp.zeros_like(acc_ref)
        carry_ref[...] = jnp.zeros_like(carry_ref)

    own = (lax.broadcasted_iota(jnp.int32, (SB_HEADS, hd), 1) // SB_HEAD_DIM
           == lax.broadcasted_iota(jnp.int32, (SB_HEADS, hd), 0))
    q8 = jnp.where(own, jnp.broadcast_to(q_ref[...], (SB_HEADS, hd)), 0.0).astype(BF16)
    sufm = _suffix_matrix(page, True)
    args, pieces = [], []
    for g in range(npg):
        kt = k_refs[g][...].reshape(hd, page).astype(BF16)
        z = _dot(q8, kt) + bias_ref[...]
        sp = _softplus(z)
        args.append(z - sp)
        pieces.append(sp)
    sums = _dot_exact_rhs(_split2(jnp.concatenate(pieces, axis=0)), sufm)
    acc = acc_ref[...]
    run = carry_ref[...]
    for g in range(npg):
        rows = slice(g * SB_HEADS, (g + 1) * SB_HEADS)
        wgt = jnp.exp(args[g] - (sums[rows, :page] + run))
        run = run + sums[rows, page:]
        vt = v_refs[g][...].reshape(hd, page).astype(BF16)
        acc = acc + _dot_nt(wgt.astype(BF16), vt)
    acc_ref[...] = acc
    carry_ref[...] = run

    @pl.when(p == pl.num_programs(1) - 1)
    def _():
        o_ref[...] = jnp.sum(jnp.where(own, acc, 0.0), axis=0, keepdims=True)


def _sb_attn_decode(q, cache_kt, cache_vt, layer, page_table, bias):
    bsz, hd = q.shape
    n_pages = page_table.shape[1]
    page = cache_kt.shape[4]
    assert page == LANES
    npg = _DEC_PAGES if n_pages % _DEC_PAGES == 0 else 1

    def page_spec(g):
        return pl.BlockSpec((None, None, SB_HEADS, SB_HEAD_DIM, page),
                            lambda b, p, pt: (layer, pt[b * n_pages + n_pages - 1 - (p * npg + g)], 0, 0, 0))

    per_b = pl.BlockSpec((None, 1, hd), lambda b, p, pt: (b, 0, 0))
    grid_spec = pltpu.PrefetchScalarGridSpec(
        num_scalar_prefetch=1,
        grid=(bsz, n_pages // npg),
        in_specs=[pl.BlockSpec((SB_HEADS, 1), lambda b, p, pt: (0, 0)), per_b]
                 + [page_spec(g) for g in range(npg)] * 2,
        out_specs=per_b,
        scratch_shapes=[pltpu.VMEM((SB_HEADS, hd), F32), pltpu.VMEM((SB_HEADS, LANES), F32)],
    )
    return pl.pallas_call(
        functools.partial(_sb_decode_body, page=page, npg=npg),
        grid_spec=grid_spec,
        out_shape=jax.ShapeDtypeStruct((bsz, 1, hd), F32),
        compiler_params=_cparams("parallel", "arbitrary"),
        name="sb_attn_decode",
    )(page_table.reshape(-1), bias.reshape(SB_HEADS, 1), q.reshape(bsz, 1, hd),
      *([cache_kt] * npg), *([cache_vt] * npg))


def _out2_body(x_ref, a_ref, o_ref, wa_ref, wo_ref, y_ref):
    y_ref[...] = (x_ref[...] + _dot(a_ref[...].astype(BF16), wa_ref[...])
                  + _dot(o_ref[...].astype(BF16), wo_ref[...]))


def _out2(x, a, o, wa, wo):
    t, d = x.shape
    tm = _token_tile(t, 512)
    row = lambda n: pl.BlockSpec((tm, n), lambda i: (i, 0))
    return pl.pallas_call(
        _out2_body,
        grid=(t // tm,),
        in_specs=[row(d), row(a.shape[1]), row(o.shape[1]), _resident(wa.shape), _resident(wo.shape)],
        out_specs=row(d),
        out_shape=jax.ShapeDtypeStruct((t, d), F32),
        compiler_params=_cparams("parallel"),
        name="ab_out",
    )(x, a, o, wa, wo)


def _out1_body(x_ref, y_ref, w_ref, o_ref):
    o_ref[...] = x_ref[...] + _dot(y_ref[...].astype(BF16), w_ref[...])


def _out1(x, y, w):
    t, d = x.shape
    tm = _token_tile(t, 512)
    row = lambda n: pl.BlockSpec((tm, n), lambda i: (i, 0))
    return pl.pallas_call(
        _out1_body,
        grid=(t // tm,),
        in_specs=[row(d), row(y.shape[1]), _resident(w.shape)],
        out_specs=row(d),
        out_shape=jax.ShapeDtypeStruct((t, d), F32),
        compiler_params=_cparams("parallel"),
        name="m_out",
    )(x, y, w)


def _m_in_body(x_ref, nw_ref, wz_ref, wx_ref, wdt_ref, wdtt_ref, z_ref, xbc_ref, dt_ref, dtt_ref):
    h = _rms_bf16(x_ref[...], nw_ref[...])
    z_ref[...] = _dot(h, wz_ref[...])
    xbc_ref[...] = _dot(h, wx_ref[...])
    dt_ref[...] = _dot(h, wdt_ref[...])
    dtt_ref[...] = _dot_nt(wdtt_ref[...], h)


def _m_in(x, nw, wz, wx, wdt, wdtt):
    t, d = x.shape
    tm = _token_tile(t, 512)
    nh = wdt.shape[1]
    row = lambda n: pl.BlockSpec((tm, n), lambda i: (i, 0))
    return pl.pallas_call(
        _m_in_body,
        grid=(t // tm,),
        in_specs=[row(d), _resident((1, d)), _resident(wz.shape), _resident(wx.shape),
                  _resident(wdt.shape), _resident(wdtt.shape)],
        out_specs=[row(wz.shape[1]), row(wx.shape[1]), row(nh), pl.BlockSpec((nh, tm), lambda i: (0, i))],
        out_shape=[jax.ShapeDtypeStruct((t, wz.shape[1]), F32), jax.ShapeDtypeStruct((t, wx.shape[1]), F32),
                   jax.ShapeDtypeStruct((t, nh), F32), jax.ShapeDtypeStruct((nh, t), F32)],
        compiler_params=_cparams("parallel"),
        name="m_in",
    )(x, nw, wz, wx, wdt, wdtt)


_MCONV_LANES = 512


def _m_conv_body(xbc_ref, w_ref, b_ref, x_ref, bm_ref, cm_ref, cn_ref, ext_ref, *, tt, d_inner, gn):
    t = pl.program_id(1)
    halo = SUBLANES
    nprev = M_CONV_W - 1

    @pl.when(t == 0)
    def _():
        ext_ref[0:halo, :] = jnp.zeros((halo, ext_ref.shape[1]), F32)

    ext_ref[halo:halo + tt, :] = xbc_ref[...]
    first = halo - nprev
    cdim = ext_ref.shape[1]
    for lc in range(cdim // _MCONV_LANES):
        ls = slice(lc * _MCONV_LANES, (lc + 1) * _MCONV_LANES)
        for r in range(tt // _CONV_ROWS):
            span = _CONV_ROWS + halo
            win = ext_ref[r * _CONV_ROWS:r * _CONV_ROWS + span, ls]
            acc = w_ref[nprev:M_CONV_W, ls] * win[halo:span]
            for j in range(nprev):
                acc = acc + w_ref[j:j + 1, ls] * pltpu.roll(win, span - (first + j), axis=0)[0:_CONV_ROWS]
            val = _silu(acc + b_ref[:, ls])
            rows = slice(r * _CONV_ROWS, (r + 1) * _CONV_ROWS)
            lo = lc * _MCONV_LANES
            if lo < d_inner:
                x_ref[rows, lo:lo + _MCONV_LANES] = val
            elif lo < d_inner + gn:
                bm_ref[rows, lo - d_inner:lo - d_inner + _MCONV_LANES] = val
            else:
                cm_ref[rows, lo - d_inner - gn:lo - d_inner - gn + _MCONV_LANES] = val.astype(BF16)

    @pl.when(t == pl.num_programs(1) - 1)
    def _():
        cn_ref[...] = ext_ref[halo + tt - nprev:halo + tt, :]

    ext_ref[0:halo, :] = ext_ref[tt:tt + halo, :]


def _m_conv_prompt(xbc, bsz, w, b, d_inner):
    t_all, cdim = xbc.shape
    gn = M_GROUPS * M_D_STATE
    assert gn == _MCONV_LANES and d_inner % _MCONV_LANES == 0
    t = t_all // bsz
    tt = _token_tile(t, 256)
    nt = t // tt
    row = lambda n: pl.BlockSpec((tt, n), lambda i, j: (i * nt + j, 0))
    return pl.pallas_call(
        functools.partial(_m_conv_body, tt=tt, d_inner=d_inner, gn=gn),
        grid=(bsz, nt),
        in_specs=[row(cdim), _resident(w.shape), _resident((1, cdim))],
        out_specs=[row(d_inner), row(gn), row(gn),
                   pl.BlockSpec((None, M_CONV_W - 1, cdim), lambda i, j: (i, 0, 0))],
        out_shape=[jax.ShapeDtypeStruct((t_all, d_inner), F32), jax.ShapeDtypeStruct((t_all, gn), F32),
                   jax.ShapeDtypeStruct((t_all, gn), BF16),
                   jax.ShapeDtypeStruct((bsz, M_CONV_W - 1, cdim), F32)],
        scratch_shapes=[pltpu.VMEM((SUBLANES + tt, cdim), F32)],
        compiler_params=_cparams("parallel", "arbitrary"),
        name="m_conv_prompt",
    )(xbc, w, b)


_EXPAND_PIECES = 3


def _head_expand_matrix(nh, width):
    h = lax.broadcasted_iota(jnp.int32, (_EXPAND_PIECES * nh, nh * width), 0) % nh
    c = lax.broadcasted_iota(jnp.int32, (_EXPAND_PIECES * nh, nh * width), 1)
    return jnp.where(c // width == h, 1.0, 0.0).astype(BF16)


def _expand_heads(v, expand):
    return _dot(jnp.concatenate(_split3(v), axis=1), expand)


def _gate_norm(y, z, nw, d_inner):
    y = y * _silu(z)
    gw = d_inner // M_GROUPS
    parts = []
    for g in range(M_GROUPS):
        yg = y[:, g * gw:(g + 1) * gw]
        parts.append(yg * lax.rsqrt(jnp.mean(yg * yg, axis=-1, keepdims=True) + EPS))
    return jnp.concatenate(parts, axis=1) * nw


def _ssd_body(x_ref, bm_ref, cm_ref, z_ref, dt_ref, dtt_ref, dtb_ref, dtbt_ref, a_ref, at_ref, de_ref, nw_ref,
              y_ref, hT_ref, h_ref, yd_ref, *, nh, d_inner):
    c = pl.program_id(1)
    L = SSD_CHUNK
    hpg = nh // M_GROUPS
    gw = hpg * M_HEAD_DIM

    @pl.when(c == 0)
    def _():
        h_ref[...] = jnp.zeros_like(h_ref)

    dt = _softplus(dt_ref[...] + dtb_ref[...])
    dtt = _softplus(dtt_ref[...] + dtbt_ref[...])
    row = lax.broadcasted_iota(jnp.int32, (L, L), 0)
    col = lax.broadcasted_iota(jnp.int32, (L, L), 1)
    causal = col <= row
    tri = jnp.where(causal, 1.0, 0.0).astype(BF16)
    trit = jnp.where(row <= col, 1.0, 0.0).astype(BF16)
    dta_hi, dta_lo = _split2(dt * a_ref[...])
    acs = _dot(tri, dta_hi) + _dot(tri, dta_lo)
    acst = _dot_exact_rhs(_split2(dtt * at_ref[...]), trit)
    acs_last = acs[L - 1:L, :]
    expand = _head_expand_matrix(nh, M_HEAD_DIM)
    dt_e = _expand_heads(dt, expand)
    eacs_e = _expand_heads(jnp.exp(acs), expand)
    eend_e = _expand_heads(jnp.exp(acs_last - acs), expand)
    etot_e = _expand_heads(jnp.broadcast_to(jnp.exp(acs_last), (SUBLANES, nh)), expand)[0:1, :]

    x = x_ref[...]
    dtx = x * dt_e
    dtx_b = dtx.astype(BF16)
    dtxe_b = (dtx * eend_e).astype(BF16)
    lane = lax.broadcasted_iota(jnp.int32, (1, LANES), 1)
    assert 2 * M_HEAD_DIM == LANES and hpg % 2 == 0
    for g in range(M_GROUPS):
        ns = slice(g * M_D_STATE, (g + 1) * M_D_STATE)
        gs = slice(g * gw, (g + 1) * gw)
        bg = bm_ref[:, ns]
        cg = cm_ref[:, ns]
        cb = _dot_nt(cg, bg.astype(BF16))
        hg = h_ref[:, gs]
        yd_ref[:, gs] = _dot(cg, hg.astype(BF16)) * eacs_e[:, gs]
        h_ref[:, gs] = hg * etot_e[:, gs] + _dot(bg.T.astype(BF16), dtxe_b[:, gs])
        for hp in range(hpg // 2):
            pair = g * hpg + 2 * hp
            ps = slice(pair * M_HEAD_DIM, (pair + 2) * M_HEAD_DIM)
            ms = []
            for head in (pair, pair + 1):
                dec = jnp.exp(jnp.where(causal, acs[:, head:head + 1] - acst[head:head + 1, :], -jnp.inf))
                ms.append((cb * dec).astype(BF16))
            dp = dtx_b[:, ps]
            rhs = jnp.concatenate([jnp.where(lane < M_HEAD_DIM, dp, jnp.zeros_like(dp)),
                                   jnp.where(lane >= M_HEAD_DIM, dp, jnp.zeros_like(dp))], axis=0)
            yd_ref[:, ps] += _dot(jnp.concatenate(ms, axis=1), rhs)

    y = yd_ref[...] + de_ref[...] * x
    y_ref[...] = _gate_norm(y, z_ref[...], nw_ref[...], d_inner).astype(BF16)

    @pl.when(c == pl.num_programs(1) - 1)
    def _():
        hT_ref[...] = h_ref[...].T


def _ssd_prompt(x, bm, cm, z, dt, dtt, dtb, a, d_e, nw, bsz):
    t_all, d_inner = x.shape
    nh = dt.shape[1]
    gn = bm.shape[1]
    t = t_all // bsz
    L = SSD_CHUNK
    assert t % L == 0
    nc = t // L
    row = lambda n: pl.BlockSpec((L, n), lambda b, c: (b * nc + c, 0))
    return pl.pallas_call(
        functools.partial(_ssd_body, nh=nh, d_inner=d_inner),
        grid=(bsz, nc),
        in_specs=[row(d_inner), row(gn), row(gn), row(d_inner), row(nh),
                  pl.BlockSpec((nh, L), lambda b, c: (0, b * nc + c)),
                  _resident((1, nh)), _resident((nh, 1)), _resident((1, nh)), _resident((nh, 1)),
                  _resident((1, d_inner)), _resident((1, d_inner))],
        out_specs=[row(d_inner), pl.BlockSpec((None, d_inner, M_D_STATE), lambda b, c: (b, 0, 0))],
        out_shape=[jax.ShapeDtypeStruct((t_all, d_inner), BF16),
                   jax.ShapeDtypeStruct((bsz, d_inner, M_D_STATE), F32)],
        scratch_shapes=[pltpu.VMEM((M_D_STATE, d_inner), F32), pltpu.VMEM((L, d_inner), F32)],
        compiler_params=_cparams("parallel", "arbitrary"),
        name="ssd_prompt",
    )(x, bm, cm, z, dt, dtt, dtb.reshape(1, nh), dtb.reshape(nh, 1), a.reshape(1, nh), a.reshape(nh, 1), d_e, nw)


def _m_step_body(cprev_ref, xbc_ref, z_ref, dt_ref, st_ref, w_ref, b_ref, dtb_ref, a_ref, de_ref, nw_ref,
                 y_ref, cn_ref, hn_ref, *, nh, d_inner):
    nprev = M_CONV_W - 1
    gn = M_GROUPS * M_D_STATE
    gw = d_inner // M_GROUPS
    new = xbc_ref[...]
    acc = w_ref[nprev:M_CONV_W, :] * new + b_ref[...]
    for j in range(nprev):
        acc = acc + w_ref[j:j + 1, :] * cprev_ref[j:j + 1, :]
    cn_ref[0:nprev - 1, :] = cprev_ref[1:nprev, :]
    cn_ref[nprev - 1:nprev, :] = new
    xbc = _silu(acc)
    x = xbc[:, :d_inner]

    dt = _softplus(dt_ref[...] + dtb_ref[...])
    expand = _head_expand_matrix(nh, M_HEAD_DIM)
    rows8 = lambda v: jnp.broadcast_to(v, (SUBLANES, nh))
    dt_e = _expand_heads(rows8(dt), expand)[0:1, :]
    da_e = _expand_heads(rows8(jnp.exp(dt * a_ref[...])), expand)[0:1, :]
    dtx = x * dt_e

    rowi = lax.broadcasted_iota(jnp.int32, (M_D_STATE, M_D_STATE), 0)
    b_mat = jnp.zeros((M_D_STATE, M_D_STATE), F32)
    c_mat = jnp.zeros((M_D_STATE, M_D_STATE), F32)
    for g in range(M_GROUPS):
        b_row = xbc[:, d_inner + g * M_D_STATE:d_inner + (g + 1) * M_D_STATE]
        c_row = xbc[:, d_inner + gn + g * M_D_STATE:d_inner + gn + (g + 1) * M_D_STATE]
        b_mat = jnp.where(rowi == g, b_row, b_mat)
        c_mat = jnp.where(rowi == g, c_row, c_mat)
    b_cols = b_mat.T
    c_cols = c_mat.T

    ht = st_ref[...].T
    hn_parts, y_parts = [], []
    for g in range(M_GROUPS):
        gs = slice(g * gw, (g + 1) * gw)
        hn = ht[:, gs] * da_e[:, gs] + b_cols[:, g:g + 1] * dtx[:, gs]
        hn_parts.append(hn)
        y_parts.append(jnp.sum(c_cols[:, g:g + 1] * hn, axis=0, keepdims=True))
    hn_ref[...] = jnp.concatenate(hn_parts, axis=1).T
    y = jnp.concatenate(y_parts, axis=1) + de_ref[...] * x
    y_ref[...] = _gate_norm(y, z_ref[...], nw_ref[...], d_inner)


def _m_step(conv_prev, xbc, z, dt, state, w, b, dtb, a, d_e, nw):
    bsz, nprev, cdim = conv_prev.shape
    d_inner = z.shape[1]
    nh = dt.shape[1]
    per_b = lambda r, n: pl.BlockSpec((None, r, n), lambda i: (i, 0, 0))
    return pl.pallas_call(
        functools.partial(_m_step_body, nh=nh, d_inner=d_inner),
        grid=(bsz,),
        in_specs=[per_b(nprev, cdim), per_b(1, cdim), per_b(1, d_inner), per_b(1, nh), per_b(d_inner, M_D_STATE),
                  _resident(w.shape), _resident((1, cdim)), _resident((1, nh)), _resident((1, nh)),
                  _resident((1, d_inner)), _resident((1, d_inner))],
        out_specs=[per_b(1, d_inner), per_b(nprev, cdim), per_b(d_inner, M_D_STATE)],
        out_shape=[jax.ShapeDtypeStruct((bsz, 1, d_inner), F32),
                   jax.ShapeDtypeStruct((bsz, nprev, cdim), F32),
                   jax.ShapeDtypeStruct((bsz, d_inner, M_D_STATE), F32)],
        compiler_params=_cparams("parallel"),
        name="m_step",
    )(conv_prev, xbc.reshape(bsz, 1, cdim), z.reshape(bsz, 1, d_inner), dt.reshape(bsz, 1, nh),
      state.reshape(bsz, d_inner, M_D_STATE), w, b, dtb.reshape(1, nh), a.reshape(1, nh), d_e, nw)


def kernel(x_prompt, x_sample, cache_sb_k, cache_sb_v, state_conv_a, state_mamba_conv, state_ssm, page_table,
           ffn1_norm, ffn1_gate, ffn1_up, ffn1_down, mix_norm, ffn2_norm, ffn2_gate, ffn2_up, ffn2_down,
           final_norm, ab_in_proj, conv_a_w, conv_a_b, conv_a_ln_g, conv_a_ln_b, sb_beta_bias, ab_out_proj,
           m_in_proj, m_conv_w, m_conv_b, m_dt_bias, m_A_log, m_D, m_norm, m_out_proj):
    bp, seq, d = x_prompt.shape
    bs = x_sample.shape[0]
    depth = ffn1_norm.shape[0]
    hd = SB_HEADS * SB_HEAD_DIM
    conv_ch = conv_a_w.shape[2]
    nh = m_dt_bias.shape[1]
    d_inner = nh * M_HEAD_DIM
    conv_dim = m_conv_w.shape[2]

    bf = lambda w: w.astype(BF16)
    f1g, f1u, f1d = bf(ffn1_gate), bf(ffn1_up), bf(ffn1_down)
    f2g, f2u, f2d = bf(ffn2_gate), bf(ffn2_up), bf(ffn2_down)
    ab_in_w, ab_out_w = bf(ab_in_proj), bf(ab_out_proj)
    m_in_w, m_out_w = bf(m_in_proj), bf(m_out_proj)
    fw = final_norm.reshape(1, d)
    cache_kt = jnp.transpose(cache_sb_k, (0, 1, 3, 4, 2))
    cache_vt = jnp.transpose(cache_sb_v, (0, 1, 3, 4, 2))

    xp = x_prompt.reshape(bp * seq, d)
    xs = x_sample.reshape(bs, d)
    outs = {k: [] for k in ("kp", "vp", "cap", "mcp", "ssp", "ks", "vs", "cas", "mcs", "sss")}
    n1 = ffn1_norm.reshape(depth, 1, d)
    n2 = ffn2_norm.reshape(depth, 1, d)
    for layer in range(depth):
        w1 = (n1, f1g, f1u, f1d, fw, layer)
        xp = _ffn(xp, *w1)
        xs = _ffn(xs, *w1)
        mnw = mix_norm[layer].reshape(1, d)
        if layer % 2 == 0:
            e = layer // 2
            cw = (conv_a_w[e], conv_a_b[e].reshape(1, conv_ch), conv_a_ln_g[e].reshape(1, conv_ch),
                  conv_a_ln_b[e].reshape(1, conv_ch))
            wa, wo = ab_out_w[e][:conv_ch], ab_out_w[e][conv_ch:]
            u, qb, kt, vt, kb, vb = _ab_in(xp, mnw, ab_in_w[e], bsz=bp)
            a_out, cvp = _conv_a_prompt(u, bp, *cw)
            o = _sb_attn_prompt(qb, kb, vb, sb_beta_bias[e], bp)
            xp = _out2(xp, a_out, o, wa, wo)
            outs["kp"].append(kt)
            outs["vp"].append(vt)
            outs["cap"].append(cvp)

            u, qb, k, v, _, _ = _ab_in(xs, mnw, ab_in_w[e])
            a_out, cvs = _conv_a_sample(state_conv_a[e], u, *cw)
            o = _sb_attn_decode(qb.astype(F32), cache_kt, cache_vt, e, page_table, sb_beta_bias[e])
            xs = _out2(xs, a_out.reshape(bs, conv_ch), o.reshape(bs, hd), wa, wo)
            outs["ks"].append(k.reshape(bs, 1, SB_HEADS, SB_HEAD_DIM))
            outs["vs"].append(v.reshape(bs, 1, SB_HEADS, SB_HEAD_DIM))
            outs["cas"].append(cvs)
        else:
            o = layer // 2
            w_in = m_in_w[o]
            wz, wx, wdt = w_in[:, :d_inner], w_in[:, d_inner:d_inner + conv_dim], w_in[:, d_inner + conv_dim:]
            wdtt = wdt.T
            a = -jnp.exp(m_A_log[o])
            d_e = jnp.repeat(m_D[o], M_HEAD_DIM).reshape(1, d_inner)
            nw = m_norm[o].reshape(1, d_inner)
            cb = m_conv_b[o].reshape(1, conv_dim)

            z, xbc, dt, dtt = _m_in(xp, mnw, wz, wx, wdt, wdtt)
            x, bm, cm, mcp = _m_conv_prompt(xbc, bp, m_conv_w[o], cb, d_inner)
            y, hT = _ssd_prompt(x, bm, cm, z, dt, dtt, m_dt_bias[o], a, d_e, nw, bp)
            xp = _out1(xp, y, m_out_w[o])
            outs["mcp"].append(mcp)
            outs["ssp"].append(hT.reshape(bp, nh, M_HEAD_DIM, M_D_STATE))

            z, xbc, dt, _ = _m_in(xs, mnw, wz, wx, wdt, wdtt)
            y, mcs, hn = _m_step(state_mamba_conv[o], xbc, z, dt, state_ssm[o], m_conv_w[o], cb,
                                 m_dt_bias[o], a, d_e, nw)
            xs = _out1(xs, y.reshape(bs, d_inner), m_out_w[o])
            outs["mcs"].append(mcs)
            outs["sss"].append(hn.reshape(bs, nh, M_HEAD_DIM, M_D_STATE))
        last = layer == depth - 1
        w2 = (n2, f2g, f2u, f2d, fw, layer)
        xp = _ffn(xp, *w2, final=last)
        xs = _ffn(xs, *w2, final=last)
    st = lambda key: jnp.stack(outs[key])

    def token_major(a):
        return jnp.transpose(a.reshape(a.shape[0], bp, SB_HEADS, SB_HEAD_DIM, seq), (0, 1, 4, 2, 3))

    return (xp.reshape(bp, seq, d), xs.reshape(bs, 1, d),
            token_major(st("kp")), token_major(st("vp")), st("cap"), st("mcp"), st("ssp"),
            st("ks"), st("vs"), st("cas"), st("mcs"), st("sss"))
```

```python
import functools

import jax
import jax.numpy as jnp
from jax import lax
from jax.experimental import pallas as pl
from jax.experimental.pallas import tpu as pltpu

F32 = jnp.float32
BF16 = jnp.bfloat16
EPS = 1e-6

LANES = 128
SUBLANES = 8
VMEM_LIMIT_BYTES = 56 * 1024 * 1024

SB_HEADS = 8
SB_HEAD_DIM = 64
CONV_W = 31
M_HEAD_DIM = 64
M_GROUPS = 4
M_D_STATE = 128
M_CONV_W = 4
SSD_CHUNK = 128


def _cparams(*sem):
    return pltpu.CompilerParams(dimension_semantics=sem, vmem_limit_bytes=VMEM_LIMIT_BYTES)


def _resident(shape):
    nd = len(shape)
    return pl.BlockSpec(shape, lambda *_: (0,) * nd, pipeline_mode=pl.Buffered(1))


def _resident_layer(shape, layer):
    nd = len(shape)
    return pl.BlockSpec((None,) + tuple(shape[1:]), lambda *_: (layer,) + (0,) * (nd - 1),
                        pipeline_mode=pl.Buffered(1))


def _rms_bf16(x, w):
    y = x * lax.rsqrt(jnp.mean(x * x, axis=-1, keepdims=True) + EPS)
    return (y * w).astype(BF16)


def _dot(a, b):
    return jnp.dot(a, b, preferred_element_type=F32)


def _dot_nt(a, b):
    return lax.dot_general(a, b, (((1,), (1,)), ((), ())), preferred_element_type=F32)


def _split2(v):
    hi = v.astype(BF16)
    lo = (v - hi.astype(F32)).astype(BF16)
    return hi, lo


def _split3(v):
    hi = v.astype(BF16)
    r = v - hi.astype(F32)
    mid = r.astype(BF16)
    lo = (r - mid.astype(F32)).astype(BF16)
    return hi, mid, lo


def _dot_exact_rhs(parts, m):
    acc = _dot(parts[0], m)
    for p in parts[1:]:
        acc = acc + _dot(p, m)
    return acc


def _softplus(z):
    return jnp.maximum(z, 0.0) + jnp.log1p(jnp.exp(-jnp.abs(z)))


def _silu(v):
    return v * jax.nn.sigmoid(v)


def _token_tile(t, want):
    return want if t % want == 0 else t


def _ffn_body(x_ref, nw_ref, wg_ref, wu_ref, wd_ref, fw_ref, o_ref, *, final):
    x = x_ref[...]
    h = _rms_bf16(x, nw_ref[...])
    g = _dot(h, wg_ref[...])
    u = _dot(h, wu_ref[...])
    a = (_silu(g) * u).astype(BF16)
    y = x + 0.5 * _dot(a, wd_ref[...])
    if final:
        y = y * lax.rsqrt(jnp.mean(y * y, axis=-1, keepdims=True) + EPS) * fw_ref[...]
    o_ref[...] = y


_FFN_CHUNK = 256


def _ffn_stream_body(x_ref, nw_ref, wg_ref, wu_ref, wd_ref, fw_ref, o_ref, wgb_ref, wub_ref, wdb_ref,
                     h_ref, acc_ref, *, final):
    c = pl.program_id(0)

    @pl.when(c == 0)
    def _():
        h_ref[...] = _rms_bf16(x_ref[...], nw_ref[...])
        acc_ref[...] = jnp.zeros_like(acc_ref)

    wg, wu, wd = wg_ref[...].astype(BF16), wu_ref[...].astype(BF16), wd_ref[...].astype(BF16)
    wgb_ref[...], wub_ref[...], wdb_ref[...] = wg, wu, wd
    h = h_ref[...]
    acc_ref[...] += _dot((_silu(_dot(h, wg)) * _dot(h, wu)).astype(BF16), wd)

    @pl.when(c == pl.num_programs(0) - 1)
    def _():
        y = x_ref[...] + 0.5 * acc_ref[...]
        if final:
            y = y * lax.rsqrt(jnp.mean(y * y, axis=-1, keepdims=True) + EPS) * fw_ref[...]
        o_ref[...] = y


def _ffn_stream(x, nw, wg, wu, wd, fw, layer, *, final=False):
    t, d = x.shape
    f = wg.shape[2]
    fc = _FFN_CHUNK
    assert f % fc == 0
    whole = lambda shape: pl.BlockSpec(shape, lambda c: (0,) * len(shape))
    return pl.pallas_call(
        functools.partial(_ffn_stream_body, final=final),
        grid=(f // fc,),
        in_specs=[whole((t, d)), pl.BlockSpec((None, 1, d), lambda c: (layer, 0, 0)),
                  pl.BlockSpec((None, d, fc), lambda c: (layer, 0, c)),
                  pl.BlockSpec((None, d, fc), lambda c: (layer, 0, c)),
                  pl.BlockSpec((None, fc, d), lambda c: (layer, c, 0)),
                  whole((1, d))],
        out_specs=[whole((t, d)), pl.BlockSpec((d, fc), lambda c: (0, c)),
                   pl.BlockSpec((d, fc), lambda c: (0, c)), pl.BlockSpec((fc, d), lambda c: (c, 0))],
        out_shape=[jax.ShapeDtypeStruct((t, d), F32), jax.ShapeDtypeStruct((d, f), BF16),
                   jax.ShapeDtypeStruct((d, f), BF16), jax.ShapeDtypeStruct((f, d), BF16)],
        scratch_shapes=[pltpu.VMEM((t, d), BF16), pltpu.VMEM((t, d), F32)],
        compiler_params=_cparams("arbitrary"),
        name="ffn_stream",
    )(x, nw, wg, wu, wd, fw)


def _ffn(x, nw, wg, wu, wd, fw, layer, *, final=False):
    t, d = x.shape
    tm = _token_tile(t, 512)
    return pl.pallas_call(
        functools.partial(_ffn_body, final=final),
        grid=(t // tm,),
        in_specs=[pl.BlockSpec((tm, d), lambda i: (i, 0)),
                  _resident_layer(nw.shape, layer), _resident(wg.shape),
                  _resident(wu.shape), _resident(wd.shape),
                  _resident((1, d))],
        out_specs=pl.BlockSpec((tm, d), lambda i: (i, 0)),
        out_shape=jax.ShapeDtypeStruct((t, d), F32),
        compiler_params=_cparams("parallel"),
        name="ffn",
    )(x, nw, wg, wu, wd, fw)


def _ab_in_body(x_ref, nw_ref, w_ref, u_ref, qb_ref, k_ref, v_ref, kb_ref, vb_ref, *, c, hd, key_minor):
    h = _rms_bf16(x_ref[...], nw_ref[...])
    p = _dot(h, w_ref[...])
    u_ref[...] = p[:, :c] * jax.nn.sigmoid(p[:, c:2 * c])
    qb_ref[...] = (p[:, 2 * c:2 * c + hd] * (SB_HEAD_DIM ** -0.5)).astype(BF16)
    k = p[:, 2 * c + hd:2 * c + 2 * hd]
    v = p[:, 2 * c + 2 * hd:]
    k_ref[...] = k.T if key_minor else k
    v_ref[...] = v.T if key_minor else v
    kb_ref[...] = k.astype(BF16)
    vb_ref[...] = v.astype(BF16)


def _ab_in(x, nw, w, bsz=None):
    t, d = x.shape
    hd = SB_HEADS * SB_HEAD_DIM
    c = (w.shape[1] - 3 * hd) // 2
    tm = _token_tile(t, 512)
    row = lambda n: pl.BlockSpec((tm, n), lambda i: (i, 0))
    if bsz is None:
        kv_spec, kv_shape = row(hd), jax.ShapeDtypeStruct((t, hd), F32)
    else:
        assert (t // bsz) % tm == 0
        nt = t // bsz // tm
        kv_spec = pl.BlockSpec((None, hd, tm), lambda i: (i // nt, 0, i % nt))
        kv_shape = jax.ShapeDtypeStruct((bsz, hd, t // bsz), F32)
    return pl.pallas_call(
        functools.partial(_ab_in_body, c=c, hd=hd, key_minor=bsz is not None),
        grid=(t // tm,),
        in_specs=[row(d), _resident((1, d)), _resident(w.shape)],
        out_specs=[row(c), row(hd), kv_spec, kv_spec, row(hd), row(hd)],
        out_shape=[jax.ShapeDtypeStruct((t, c), F32), jax.ShapeDtypeStruct((t, hd), BF16),
                   kv_shape, kv_shape,
                   jax.ShapeDtypeStruct((t, hd), BF16), jax.ShapeDtypeStruct((t, hd), BF16)],
        compiler_params=_cparams("parallel"),
        name="ab_in",
    )(x, nw, w)


_CONV_ROWS = 32
_CONV_HALO = 32


def _layernorm_silu(cv, g, b):
    mu = jnp.mean(cv, axis=-1, keepdims=True)
    d = cv - mu
    var = jnp.mean(d * d, axis=-1, keepdims=True)
    return _silu(d * lax.rsqrt(var + EPS) * g + b)


def _conv_a_body(u_ref, w_ref, b_ref, g_ref, be_ref, a_ref, cn_ref, ext_ref, sh_ref, *, tt):
    t = pl.program_id(1)

    @pl.when(t == 0)
    def _():
        ext_ref[0:_CONV_HALO, :] = jnp.zeros((_CONV_HALO, ext_ref.shape[1]), F32)

    ext_ref[_CONV_HALO:_CONV_HALO + tt, :] = u_ref[...]
    first = _CONV_HALO - (CONV_W - 1)
    taps = [list(range(s, CONV_W, SUBLANES)) for s in range(SUBLANES)]
    rows = _CONV_HALO + tt
    for s in range(SUBLANES):
        span = tt + taps[s][-1] - s
        sh_ref[s, 0:span, :] = pltpu.roll(ext_ref[...], rows - (first + s), axis=0)[0:span]
    for r in range(tt // _CONV_ROWS):
        acc = None
        for s in range(SUBLANES):
            for j in taps[s]:
                lo = r * _CONV_ROWS + j - s
                term = w_ref[j:j + 1, :] * sh_ref[s, lo:lo + _CONV_ROWS, :]
                acc = term if acc is None else acc + term
        cv = acc + b_ref[...]
        a_ref[r * _CONV_ROWS:(r + 1) * _CONV_ROWS, :] = _layernorm_silu(cv, g_ref[...], be_ref[...]).astype(BF16)

    @pl.when(t == pl.num_programs(1) - 1)
    def _():
        cn_ref[...] = ext_ref[_CONV_HALO + tt - (CONV_W - 1):_CONV_HALO + tt, :]

    ext_ref[0:_CONV_HALO, :] = ext_ref[tt:tt + _CONV_HALO, :]


def _conv_a_prompt(u, bsz, w, b, g, be):
    t_all, c = u.shape
    t = t_all // bsz
    tt = _token_tile(t, 512)
    nt = t // tt
    return pl.pallas_call(
        functools.partial(_conv_a_body, tt=tt),
        grid=(bsz, nt),
        in_specs=[pl.BlockSpec((tt, c), lambda i, j: (i * nt + j, 0)),
                  _resident(w.shape), _resident((1, c)), _resident((1, c)), _resident((1, c))],
        out_specs=[pl.BlockSpec((tt, c), lambda i, j: (i * nt + j, 0)),
                   pl.BlockSpec((None, CONV_W - 1, c), lambda i, j: (i, 0, 0))],
        out_shape=[jax.ShapeDtypeStruct((t_all, c), BF16),
                   jax.ShapeDtypeStruct((bsz, CONV_W - 1, c), F32)],
        scratch_shapes=[pltpu.VMEM((_CONV_HALO + tt, c), F32),
                        pltpu.VMEM((SUBLANES, tt + _CONV_HALO - SUBLANES, c), F32)],
        compiler_params=_cparams("parallel", "arbitrary"),
        name="conv_a_prompt",
    )(u, w, b, g, be)


def _conv_a_sample_body(prev_ref, u_ref, w_ref, b_ref, g_ref, be_ref, a_ref, cn_ref):
    nprev = CONV_W - 1
    prev = prev_ref[...]
    u = u_ref[...]
    cv = jnp.sum(w_ref[0:nprev, :] * prev, axis=0, keepdims=True) + w_ref[nprev:CONV_W, :] * u + b_ref[...]
    a_ref[...] = _layernorm_silu(cv, g_ref[...], be_ref[...])
    cn_ref[0:nprev - 1, :] = prev_ref[1:nprev, :]
    cn_ref[nprev - 1:nprev, :] = u


def _conv_a_sample(prev, u, w, b, g, be):
    bsz, nprev, c = prev.shape
    return pl.pallas_call(
        _conv_a_sample_body,
        grid=(bsz,),
        in_specs=[pl.BlockSpec((None, nprev, c), lambda i: (i, 0, 0)),
                  pl.BlockSpec((None, 1, c), lambda i: (i, 0, 0)),
                  _resident(w.shape), _resident((1, c)), _resident((1, c)), _resident((1, c))],
        out_specs=[pl.BlockSpec((None, 1, c), lambda i: (i, 0, 0)),
                   pl.BlockSpec((None, nprev, c), lambda i: (i, 0, 0))],
        out_shape=[jax.ShapeDtypeStruct((bsz, 1, c), F32),
                   jax.ShapeDtypeStruct((bsz, nprev, c), F32)],
        compiler_params=_cparams("parallel"),
        name="conv_a_sample",
    )(prev, u.reshape(bsz, 1, c), w, b, g, be)


def _suffix_matrix(tk, with_total):
    cols = tk + LANES if with_total else tk
    j = lax.broadcasted_iota(jnp.int32, (tk, cols), 0)
    s = lax.broadcasted_iota(jnp.int32, (tk, cols), 1)
    return jnp.where((j > s) | (s >= tk), 1.0, 0.0).astype(BF16)


def _neg_abs(x):
    return lax.bitcast_convert_type(lax.bitcast_convert_type(x, jnp.uint32) | jnp.uint32(0x80000000), F32)


_SB_STRIP = 32


def _sb_weights(z, bias, sufm, carry, diagonal):
    tq, tk = z.shape
    strips = [slice(r * _SB_STRIP, (r + 1) * _SB_STRIP) for r in range(tq // _SB_STRIP)]

    def admitted(r):
        rowi = lax.broadcasted_iota(jnp.int32, (_SB_STRIP, tk), 0) + r * _SB_STRIP
        coli = lax.broadcasted_iota(jnp.int32, (_SB_STRIP, tk), 1)
        return coli < rowi

    sp_parts, arg_parts, first = [], [], []
    for r, rows in enumerate(strips):
        zs = z[rows] + bias
        sp = jnp.maximum(zs, 0.0) + jnp.log(1.0 + jnp.exp(_neg_abs(zs)))
        arg_parts.append(zs - sp)
        if diagonal:
            sp = jnp.where(admitted(r), sp, 0.0)
        first.append(sp[:, 0:1])
        sp_parts.append(sp.astype(BF16))
    sums = _dot(jnp.concatenate(sp_parts, axis=0), sufm)
    w_parts = []
    for r, rows in enumerate(strips):
        cr = carry[rows]
        after = jnp.concatenate([sums[rows, c * LANES:(c + 1) * LANES] + cr for c in range(tk // LANES)], axis=1)
        wgt = jnp.exp(arg_parts[r] - after)
        if diagonal:
            wgt = jnp.where(admitted(r), wgt, 0.0)
        w_parts.append(wgt.astype(BF16))
    total = sums[:, 0:1] + jnp.concatenate(first, axis=0)
    return jnp.concatenate(w_parts, axis=0), carry + jnp.broadcast_to(total, carry.shape)


def _sb_prompt_body(bias_ref, q_ref, k_ref, v_ref, o_ref, qm_ref, *refs, tq):
    i = pl.program_id(1)
    tk = tq
    sufm = _suffix_matrix(tk, False)
    lane = lax.broadcasted_iota(jnp.int32, (1, LANES), 1)
    sels = [lane < SB_HEAD_DIM, lane >= SB_HEAD_DIM]
    pair_lanes = [slice(hp * LANES, (hp + 1) * LANES) for hp in range(SB_HEADS // 2)]
    for head in range(SB_HEADS):
        qp = q_ref[:, pair_lanes[head // 2]]
        qm_ref[head] = jnp.where(sels[head % 2], qp, jnp.zeros_like(qp))

    per_head = lambda k: refs[k * SB_HEADS:(k + 1) * SB_HEADS]
    z_refs, w_refs, carry_refs = per_head(0), per_head(1), per_head(2)
    acc_refs = refs[3 * SB_HEADS:]
    for r in carry_refs + acc_refs:
        r[...] = jnp.zeros_like(r)

    def logits(kj):
        start = pl.multiple_of(kj * tk, tk)
        for head in range(SB_HEADS):
            z_refs[head][...] = _dot_nt(qm_ref[head], k_ref[pl.ds(start, tk), pair_lanes[head // 2]])

    def weights(diagonal):
        for head in range(SB_HEADS):
            w_refs[head][...], carry_refs[head][...] = _sb_weights(
                z_refs[head][...], bias_ref[head], sufm, carry_refs[head][...], diagonal)

    def apply(kj):
        start = pl.multiple_of(kj * tk, tk)
        for hp in range(SB_HEADS // 2):
            vb = v_ref[pl.ds(start, tk), pair_lanes[hp]]
            acc_refs[hp][...] += (
                _dot(w_refs[2 * hp][...], jnp.where(sels[0], vb, jnp.zeros_like(vb)))
                + _dot(w_refs[2 * hp + 1][...], jnp.where(sels[1], vb, jnp.zeros_like(vb))))

    logits(i)
    weights(True)
    logits(jnp.maximum(i - 1, 0))

    def step(jj, _):
        kj = i - jj
        apply(kj + 1)
        weights(False)
        logits(jnp.maximum(kj - 1, 0))
        return 0

    lax.fori_loop(1, i + 1, step, 0)
    apply(0)
    for hp in range(SB_HEADS // 2):
        o_ref[:, pair_lanes[hp]] = acc_refs[hp][...].astype(o_ref.dtype)


def _sb_attn_prompt(qb, kb, vb, bias, bsz):
    t_all, hd = qb.shape
    t = t_all // bsz
    tq = _token_tile(t, 256)
    nq = t // tq
    return pl.pallas_call(
        functools.partial(_sb_prompt_body, tq=tq),
        grid=(bsz, nq),
        in_specs=[pl.BlockSpec(memory_space=pltpu.SMEM),
                  pl.BlockSpec((tq, hd), lambda b, i: (b * nq + i, 0)),
                  pl.BlockSpec((t, hd), lambda b, i: (b, 0)),
                  pl.BlockSpec((t, hd), lambda b, i: (b, 0))],
        out_specs=pl.BlockSpec((tq, hd), lambda b, i: (b * nq + i, 0)),
        out_shape=jax.ShapeDtypeStruct((t_all, hd), BF16),
        scratch_shapes=([pltpu.VMEM((SB_HEADS, tq, LANES), BF16)]
                        + [pltpu.VMEM((tq, tq), F32)] * SB_HEADS
                        + [pltpu.VMEM((tq, tq), BF16)] * SB_HEADS
                        + [pltpu.VMEM((tq, LANES), F32)] * SB_HEADS
                        + [pltpu.VMEM((tq, LANES), F32)] * (SB_HEADS // 2)),
        compiler_params=_cparams("parallel", "arbitrary"),
        name="sb_attn_prompt",
    )(bias, qb, kb, vb)


_DEC_PAGES = 32


def _sb_decode_body(pt_ref, bias_ref, q_ref, *refs, page, npg):
    k_refs, v_refs = refs[:npg], refs[npg:2 * npg]
    o_ref, acc_ref, carry_ref = refs[2 * npg:]
    p = pl.program_id(1)
    hd = SB_HEADS * SB_HEAD_DIM

    @pl.when(p == 0)
    def _():
        acc_ref[...] = jnp.zeros_like(acc_ref)
        carry_ref[...] = jnp.zeros_like(carry_ref)

    own = (lax.broadcasted_iota(jnp.int32, (SB_HEADS, hd), 1) // SB_HEAD_DIM
           == lax.broadcasted_iota(jnp.int32, (SB_HEADS, hd), 0))
    q8 = jnp.where(own, jnp.broadcast_to(q_ref[...], (SB_HEADS, hd)), 0.0).astype(BF16)
    sufm = _suffix_matrix(page, True)
    args, pieces = [], []
    for g in range(npg):
        kt = k_refs[g][...].reshape(hd, page).astype(BF16)
        z = _dot(q8, kt) + bias_ref[...]
        sp = _softplus(z)
        args.append(z - sp)
        pieces.append(sp)
    sums = _dot_exact_rhs(_split2(jnp.concatenate(pieces, axis=0)), sufm)
    acc = acc_ref[...]
    run = carry_ref[...]
    for g in range(npg):
        rows = slice(g * SB_HEADS, (g + 1) * SB_HEADS)
        wgt = jnp.exp(args[g] - (sums[rows, :page] + run))
        run = run + sums[rows, page:]
        vt = v_refs[g][...].reshape(hd, page).astype(BF16)
        acc = acc + _dot_nt(wgt.astype(BF16), vt)
    acc_ref[...] = acc
    carry_ref[...] = run

    @pl.when(p == pl.num_programs(1) - 1)
    def _():
        o_ref[...] = jnp.sum(jnp.where(own, acc, 0.0), axis=0, keepdims=True)


def _sb_attn_decode(q, cache_kt, cache_vt, layer, page_table, bias):
    bsz, hd = q.shape
    n_pages = page_table.shape[1]
    page = cache_kt.shape[4]
    assert page == LANES
    npg = _DEC_PAGES if n_pages % _DEC_PAGES == 0 else 1

    def page_spec(g):
        return pl.BlockSpec((None, None, SB_HEADS, SB_HEAD_DIM, page),
                            lambda b, p, pt: (layer, pt[b * n_pages + n_pages - 1 - (p * npg + g)], 0, 0, 0))

    per_b = pl.BlockSpec((None, 1, hd), lambda b, p, pt: (b, 0, 0))
    grid_spec = pltpu.PrefetchScalarGridSpec(
        num_scalar_prefetch=1,
        grid=(bsz, n_pages // npg),
        in_specs=[pl.BlockSpec((SB_HEADS, 1), lambda b, p, pt: (0, 0)), per_b]
                 + [page_spec(g) for g in range(npg)] * 2,
        out_specs=per_b,
        scratch_shapes=[pltpu.VMEM((SB_HEADS, hd), F32), pltpu.VMEM((SB_HEADS, LANES), F32)],
    )
    return pl.pallas_call(
        functools.partial(_sb_decode_body, page=page, npg=npg),
        grid_spec=grid_spec,
        out_shape=jax.ShapeDtypeStruct((bsz, 1, hd), F32),
        compiler_params=_cparams("parallel", "arbitrary"),
        name="sb_attn_decode",
    )(page_table.reshape(-1), bias.reshape(SB_HEADS, 1), q.reshape(bsz, 1, hd),
      *([cache_kt] * npg), *([cache_vt] * npg))


def _out2_body(x_ref, a_ref, o_ref, wa_ref, wo_ref, y_ref):
    y_ref[...] = (x_ref[...] + _dot(a_ref[...].astype(BF16), wa_ref[...])
                  + _dot(o_ref[...].astype(BF16), wo_ref[...]))


def _out2(x, a, o, wa, wo):
    t, d = x.shape
    tm = _token_tile(t, 512)
    row = lambda n: pl.BlockSpec((tm, n), lambda i: (i, 0))
    return pl.pallas_call(
        _out2_body,
        grid=(t // tm,),
        in_specs=[row(d), row(a.shape[1]), row(o.shape[1]), _resident(wa.shape), _resident(wo.shape)],
        out_specs=row(d),
        out_shape=jax.ShapeDtypeStruct((t, d), F32),
        compiler_params=_cparams("parallel"),
        name="ab_out",
    )(x, a, o, wa, wo)


def _out1_body(x_ref, y_ref, w_ref, o_ref):
    o_ref[...] = x_ref[...] + _dot(y_ref[...].astype(BF16), w_ref[...])


def _out1(x, y, w):
    t, d = x.shape
    tm = _token_tile(t, 512)
    row = lambda n: pl.BlockSpec((tm, n), lambda i: (i, 0))
    return pl.pallas_call(
        _out1_body,
        grid=(t // tm,),
        in_specs=[row(d), row(y.shape[1]), _resident(w.shape)],
        out_specs=row(d),
        out_shape=jax.ShapeDtypeStruct((t, d), F32),
        compiler_params=_cparams("parallel"),
        name="m_out",
    )(x, y, w)


def _m_in_body(x_ref, nw_ref, wz_ref, wx_ref, wdt_ref, wdtt_ref, z_ref, xbc_ref, dt_ref, dtt_ref):
    h = _rms_bf16(x_ref[...], nw_ref[...])
    z_ref[...] = _dot(h, wz_ref[...])
    xbc_ref[...] = _dot(h, wx_ref[...])
    dt_ref[...] = _dot(h, wdt_ref[...])
    dtt_ref[...] = _dot_nt(wdtt_ref[...], h)


def _m_in(x, nw, wz, wx, wdt, wdtt):
    t, d = x.shape
    tm = _token_tile(t, 512)
    nh = wdt.shape[1]
    row = lambda n: pl.BlockSpec((tm, n), lambda i: (i, 0))
    return pl.pallas_call(
        _m_in_body,
        grid=(t // tm,),
        in_specs=[row(d), _resident((1, d)), _resident(wz.shape), _resident(wx.shape),
                  _resident(wdt.shape), _resident(wdtt.shape)],
        out_specs=[row(wz.shape[1]), row(wx.shape[1]), row(nh), pl.BlockSpec((nh, tm), lambda i: (0, i))],
        out_shape=[jax.ShapeDtypeStruct((t, wz.shape[1]), F32), jax.ShapeDtypeStruct((t, wx.shape[1]), F32),
                   jax.ShapeDtypeStruct((t, nh), F32), jax.ShapeDtypeStruct((nh, t), F32)],
        compiler_params=_cparams("parallel"),
        name="m_in",
    )(x, nw, wz, wx, wdt, wdtt)


_MCONV_LANES = 512


def _m_conv_tile(src_ref, lane_chunks, w_ref, b_ref, x_ref, bm_ref, cm_ref, *, tt, d_inner, gn):
    halo = SUBLANES
    nprev = M_CONV_W - 1
    first = halo - nprev
    for lc in lane_chunks:
        ls = slice(lc * _MCONV_LANES, (lc + 1) * _MCONV_LANES)
        for r in range(tt // _CONV_ROWS):
            span = _CONV_ROWS + halo
            win = src_ref[r * _CONV_ROWS:r * _CONV_ROWS + span, ls]
            acc = w_ref[nprev:M_CONV_W, ls] * win[halo:span]
            for j in range(nprev):
                acc = acc + w_ref[j:j + 1, ls] * pltpu.roll(win, span - (first + j), axis=0)[0:_CONV_ROWS]
            val = _silu(acc + b_ref[:, ls])
            rows = slice(r * _CONV_ROWS, (r + 1) * _CONV_ROWS)
            lo = lc * _MCONV_LANES
            if lo < d_inner:
                x_ref[rows, lo:lo + _MCONV_LANES] = val
            elif lo < d_inner + gn:
                bm_ref[rows, lo - d_inner:lo - d_inner + _MCONV_LANES] = val
            else:
                cm_ref[rows, lo - d_inner - gn:lo - d_inner - gn + _MCONV_LANES] = val.astype(BF16)


def _m_in_conv_body(x_ref, nw_ref, wz_ref, wx_ref, wdt_ref, wdtt_ref, cw_ref, cb_ref,
                    z_ref, dt_ref, dtt_ref, xc_ref, bm_ref, cm_ref, cn_ref, buf0_ref, buf1_ref,
                    *, tm, nt, d_inner, gn):
    j = pl.program_id(1)
    halo = SUBLANES
    bufs = (buf0_ref, buf1_ref)
    n_chunks = buf0_ref.shape[1] // _MCONV_LANES
    conv = functools.partial(_m_conv_tile, w_ref=cw_ref, b_ref=cb_ref, x_ref=xc_ref, bm_ref=bm_ref,
                             cm_ref=cm_ref, tt=tm, d_inner=d_inner, gn=gn)

    @pl.when(j == 0)
    def _():
        buf1_ref[...] = jnp.zeros_like(buf1_ref)

    for parity in range(2):
        cur, prev = bufs[parity], bufs[1 - parity]

        @pl.when((j < nt) & (j % 2 == parity))
        def _(cur=cur, prev=prev):
            cur[0:halo, :] = prev[tm:tm + halo, :]
            h = _rms_bf16(x_ref[...], nw_ref[...])
            for lc in range(n_chunks):
                ls = slice(lc * _MCONV_LANES, (lc + 1) * _MCONV_LANES)
                cur[halo:halo + tm, ls] = _dot(h, wx_ref[:, ls])
                conv(prev, [lc])
            z_ref[...] = _dot(h, wz_ref[...])
            dt_ref[...] = _dot(h, wdt_ref[...])
            dtt_ref[...] = _dot_nt(wdtt_ref[...], h)

    last = bufs[(nt - 1) % 2]

    @pl.when(j == nt)
    def _():
        conv(last, range(n_chunks))
        cn_ref[...] = last[halo + tm - (M_CONV_W - 1):halo + tm, :]


def _m_in_conv_prompt(x, bsz, nw, wz, wx, wdt, wdtt, cw, cb):
    t_all, d = x.shape
    d_inner, cdim, nh = wz.shape[1], wx.shape[1], wdt.shape[1]
    gn = M_GROUPS * M_D_STATE
    assert gn == _MCONV_LANES and d_inner % _MCONV_LANES == 0 and cdim == d_inner + 2 * gn
    t = t_all // bsz
    tm = _token_tile(t, 512)
    nt = t // tm
    proj = lambda n: pl.BlockSpec((tm, n), lambda i, j: (i * nt + jnp.minimum(j, nt - 1), 0))
    conv = lambda n: pl.BlockSpec((tm, n), lambda i, j: (i * nt + jnp.maximum(j - 1, 0), 0))
    return pl.pallas_call(
        functools.partial(_m_in_conv_body, tm=tm, nt=nt, d_inner=d_inner, gn=gn),
        grid=(bsz, nt + 1),
        in_specs=[proj(d), _resident((1, d)), _resident(wz.shape), _resident(wx.shape),
                  _resident(wdt.shape), _resident(wdtt.shape), _resident(cw.shape), _resident((1, cdim))],
        out_specs=[proj(d_inner), proj(nh),
                   pl.BlockSpec((nh, tm), lambda i, j: (0, i * nt + jnp.minimum(j, nt - 1))),
                   conv(d_inner), conv(gn), conv(gn),
                   pl.BlockSpec((None, M_CONV_W - 1, cdim), lambda i, j: (i, 0, 0))],
        out_shape=[jax.ShapeDtypeStruct((t_all, d_inner), F32), jax.ShapeDtypeStruct((t_all, nh), F32),
                   jax.ShapeDtypeStruct((nh, t_all), F32),
                   jax.ShapeDtypeStruct((t_all, d_inner), F32), jax.ShapeDtypeStruct((t_all, gn), F32),
                   jax.ShapeDtypeStruct((t_all, gn), BF16),
                   jax.ShapeDtypeStruct((bsz, M_CONV_W - 1, cdim), F32)],
        scratch_shapes=[pltpu.VMEM((SUBLANES + tm, cdim), F32)] * 2,
        compiler_params=_cparams("parallel", "arbitrary"),
        name="m_in_conv_prompt",
    )(x, nw, wz, wx, wdt, wdtt, cw, cb)


_EXPAND_PIECES = 3


def _head_expand_matrix(nh, width):
    h = lax.broadcasted_iota(jnp.int32, (_EXPAND_PIECES * nh, nh * width), 0) % nh
    c = lax.broadcasted_iota(jnp.int32, (_EXPAND_PIECES * nh, nh * width), 1)
    return jnp.where(c // width == h, 1.0, 0.0).astype(BF16)


def _expand_heads(v, expand):
    return _dot(jnp.concatenate(_split3(v), axis=1), expand)


def _gate_norm(y, z, nw, d_inner):
    y = y * _silu(z)
    gw = d_inner // M_GROUPS
    parts = []
    for g in range(M_GROUPS):
        yg = y[:, g * gw:(g + 1) * gw]
        parts.append(yg * lax.rsqrt(jnp.mean(yg * yg, axis=-1, keepdims=True) + EPS))
    return jnp.concatenate(parts, axis=1) * nw


def _ssd_body(x_ref, bm_ref, cm_ref, z_ref, dt_ref, dtt_ref, dtb_ref, dtbt_ref, a_ref, at_ref, de_ref, nw_ref,
              y_ref, hT_ref, h_ref, yd_ref, *, nh, d_inner):
    c = pl.program_id(1)
    L = SSD_CHUNK
    hpg = nh // M_GROUPS
    gw = hpg * M_HEAD_DIM

    @pl.when(c == 0)
    def _():
        h_ref[...] = jnp.zeros_like(h_ref)

    dt = _softplus(dt_ref[...] + dtb_ref[...])
    dtt = _softplus(dtt_ref[...] + dtbt_ref[...])
    row = lax.broadcasted_iota(jnp.int32, (L, L), 0)
    col = lax.broadcasted_iota(jnp.int32, (L, L), 1)
    causal = col <= row
    tri = jnp.where(causal, 1.0, 0.0).astype(BF16)
    trit = jnp.where(row <= col, 1.0, 0.0).astype(BF16)
    dta_hi, dta_lo = _split2(dt * a_ref[...])
    acs = _dot(tri, dta_hi) + _dot(tri, dta_lo)
    acst = _dot_exact_rhs(_split2(dtt * at_ref[...]), trit)
    acs_last = acs[L - 1:L, :]
    expand = _head_expand_matrix(nh, M_HEAD_DIM)
    dt_e = _expand_heads(dt, expand)
    eacs_e = _expand_heads(jnp.exp(acs), expand)
    eend_e = _expand_heads(jnp.exp(acs_last - acs), expand)
    etot_e = _expand_heads(jnp.broadcast_to(jnp.exp(acs_last), (SUBLANES, nh)), expand)[0:1, :]

    x = x_ref[...]
    dtx = x * dt_e
    dtx_b = dtx.astype(BF16)
    dtxe_b = (dtx * eend_e).astype(BF16)
    lane = lax.broadcasted_iota(jnp.int32, (1, LANES), 1)
    assert 2 * M_HEAD_DIM == LANES and hpg % 2 == 0
    for g in range(M_GROUPS):
        ns = slice(g * M_D_STATE, (g + 1) * M_D_STATE)
        gs = slice(g * gw, (g + 1) * gw)
        bg = bm_ref[:, ns]
        cg = cm_ref[:, ns]
        cb = _dot_nt(cg, bg.astype(BF16))
        hg = h_ref[:, gs]
        yd_ref[:, gs] = _dot(cg, hg.astype(BF16)) * eacs_e[:, gs]
        h_ref[:, gs] = hg * etot_e[:, gs] + _dot(bg.T.astype(BF16), dtxe_b[:, gs])
        for hp in range(hpg // 2):
            pair = g * hpg + 2 * hp
            ps = slice(pair * M_HEAD_DIM, (pair + 2) * M_HEAD_DIM)
            ms = []
            for head in (pair, pair + 1):
                dec = jnp.exp(jnp.where(causal, acs[:, head:head + 1] - acst[head:head + 1, :], -jnp.inf))
                ms.append((cb * dec).astype(BF16))
            dp = dtx_b[:, ps]
            rhs = jnp.concatenate([jnp.where(lane < M_HEAD_DIM, dp, jnp.zeros_like(dp)),
                                   jnp.where(lane >= M_HEAD_DIM, dp, jnp.zeros_like(dp))], axis=0)
            yd_ref[:, ps] += _dot(jnp.concatenate(ms, axis=1), rhs)

    y = yd_ref[...] + de_ref[...] * x
    y_ref[...] = _gate_norm(y, z_ref[...], nw_ref[...], d_inner).astype(BF16)

    @pl.when(c == pl.num_programs(1) - 1)
    def _():
        hT_ref[...] = h_ref[...].T


def _ssd_prompt(x, bm, cm, z, dt, dtt, dtb, a, d_e, nw, bsz):
    t_all, d_inner = x.shape
    nh = dt.shape[1]
    gn = bm.shape[1]
    t = t_all // bsz
    L = SSD_CHUNK
    assert t % L == 0
    nc = t // L
    row = lambda n: pl.BlockSpec((L, n), lambda b, c: (b * nc + c, 0))
    return pl.pallas_call(
        functools.partial(_ssd_body, nh=nh, d_inner=d_inner),
        grid=(bsz, nc),
        in_specs=[row(d_inner), row(gn), row(gn), row(d_inner), row(nh),
                  pl.BlockSpec((nh, L), lambda b, c: (0, b * nc + c)),
                  _resident((1, nh)), _resident((nh, 1)), _resident((1, nh)), _resident((nh, 1)),
                  _resident((1, d_inner)), _resident((1, d_inner))],
        out_specs=[row(d_inner), pl.BlockSpec((None, d_inner, M_D_STATE), lambda b, c: (b, 0, 0))],
        out_shape=[jax.ShapeDtypeStruct((t_all, d_inner), BF16),
                   jax.ShapeDtypeStruct((bsz, d_inner, M_D_STATE), F32)],
        scratch_shapes=[pltpu.VMEM((M_D_STATE, d_inner), F32), pltpu.VMEM((L, d_inner), F32)],
        compiler_params=_cparams("parallel", "arbitrary"),
        name="ssd_prompt",
    )(x, bm, cm, z, dt, dtt, dtb.reshape(1, nh), dtb.reshape(nh, 1), a.reshape(1, nh), a.reshape(nh, 1), d_e, nw)


def _m_step_body(cprev_ref, xbc_ref, z_ref, dt_ref, st_ref, w_ref, b_ref, dtb_ref, a_ref, de_ref, nw_ref,
                 y_ref, cn_ref, hn_ref, *, nh, d_inner):
    nprev = M_CONV_W - 1
    gn = M_GROUPS * M_D_STATE
    gw = d_inner // M_GROUPS
    new = xbc_ref[...]
    acc = w_ref[nprev:M_CONV_W, :] * new + b_ref[...]
    for j in range(nprev):
        acc = acc + w_ref[j:j + 1, :] * cprev_ref[j:j + 1, :]
    cn_ref[0:nprev - 1, :] = cprev_ref[1:nprev, :]
    cn_ref[nprev - 1:nprev, :] = new
    xbc = _silu(acc)
    x = xbc[:, :d_inner]

    dt = _softplus(dt_ref[...] + dtb_ref[...])
    expand = _head_expand_matrix(nh, M_HEAD_DIM)
    rows8 = lambda v: jnp.broadcast_to(v, (SUBLANES, nh))
    dt_e = _expand_heads(rows8(dt), expand)[0:1, :]
    da_e = _expand_heads(rows8(jnp.exp(dt * a_ref[...])), expand)[0:1, :]
    dtx = x * dt_e

    rowi = lax.broadcasted_iota(jnp.int32, (M_D_STATE, M_D_STATE), 0)
    b_mat = jnp.zeros((M_D_STATE, M_D_STATE), F32)
    c_mat = jnp.zeros((M_D_STATE, M_D_STATE), F32)
    for g in range(M_GROUPS):
        b_row = xbc[:, d_inner + g * M_D_STATE:d_inner + (g + 1) * M_D_STATE]
        c_row = xbc[:, d_inner + gn + g * M_D_STATE:d_inner + gn + (g + 1) * M_D_STATE]
        b_mat = jnp.where(rowi == g, b_row, b_mat)
        c_mat = jnp.where(rowi == g, c_row, c_mat)
    b_cols = b_mat.T
    c_cols = c_mat.T

    ht = st_ref[...].T
    hn_parts, y_parts = [], []
    for g in range(M_GROUPS):
        gs = slice(g * gw, (g + 1) * gw)
        hn = ht[:, gs] * da_e[:, gs] + b_cols[:, g:g + 1] * dtx[:, gs]
        hn_parts.append(hn)
        y_parts.append(jnp.sum(c_cols[:, g:g + 1] * hn, axis=0, keepdims=True))
    hn_ref[...] = jnp.concatenate(hn_parts, axis=1).T
    y = jnp.concatenate(y_parts, axis=1) + de_ref[...] * x
    y_ref[...] = _gate_norm(y, z_ref[...], nw_ref[...], d_inner)


def _m_step(conv_prev, xbc, z, dt, state, w, b, dtb, a, d_e, nw):
    bsz, nprev, cdim = conv_prev.shape
    d_inner = z.shape[1]
    nh = dt.shape[1]
    per_b = lambda r, n: pl.BlockSpec((None, r, n), lambda i: (i, 0, 0))
    return pl.pallas_call(
        functools.partial(_m_step_body, nh=nh, d_inner=d_inner),
        grid=(bsz,),
        in_specs=[per_b(nprev, cdim), per_b(1, cdim), per_b(1, d_inner), per_b(1, nh), per_b(d_inner, M_D_STATE),
                  _resident(w.shape), _resident((1, cdim)), _resident((1, nh)), _resident((1, nh)),
                  _resident((1, d_inner)), _resident((1, d_inner))],
        out_specs=[per_b(1, d_inner), per_b(nprev, cdim), per_b(d_inner, M_D_STATE)],
        out_shape=[jax.ShapeDtypeStruct((bsz, 1, d_inner), F32),
                   jax.ShapeDtypeStruct((bsz, nprev, cdim), F32),
                   jax.ShapeDtypeStruct((bsz, d_inner, M_D_STATE), F32)],
        compiler_params=_cparams("parallel"),
        name="m_step",
    )(conv_prev, xbc.reshape(bsz, 1, cdim), z.reshape(bsz, 1, d_inner), dt.reshape(bsz, 1, nh),
      state.reshape(bsz, d_inner, M_D_STATE), w, b, dtb.reshape(1, nh), a.reshape(1, nh), d_e, nw)


def kernel(x_prompt, x_sample, cache_sb_k, cache_sb_v, state_conv_a, state_mamba_conv, state_ssm, page_table,
           ffn1_norm, ffn1_gate, ffn1_up, ffn1_down, mix_norm, ffn2_norm, ffn2_gate, ffn2_up, ffn2_down,
           final_norm, ab_in_proj, conv_a_w, conv_a_b, conv_a_ln_g, conv_a_ln_b, sb_beta_bias, ab_out_proj,
           m_in_proj, m_conv_w, m_conv_b, m_dt_bias, m_A_log, m_D, m_norm, m_out_proj):
    bp, seq, d = x_prompt.shape
    bs = x_sample.shape[0]
    depth = ffn1_norm.shape[0]
    hd = SB_HEADS * SB_HEAD_DIM
    conv_ch = conv_a_w.shape[2]
    nh = m_dt_bias.shape[1]
    d_inner = nh * M_HEAD_DIM
    conv_dim = m_conv_w.shape[2]

    bf = lambda w: w.astype(BF16)
    ab_in_w, ab_out_w = bf(ab_in_proj), bf(ab_out_proj)
    m_in_w, m_out_w = bf(m_in_proj), bf(m_out_proj)
    fw = final_norm.reshape(1, d)
    cache_kt = jnp.transpose(cache_sb_k, (0, 1, 3, 4, 2))
    cache_vt = jnp.transpose(cache_sb_v, (0, 1, 3, 4, 2))

    xp = x_prompt.reshape(bp * seq, d)
    xs = x_sample.reshape(bs, d)
    outs = {k: [] for k in ("kp", "vp", "cap", "mcp", "ssp", "ks", "vs", "cas", "mcs", "sss")}
    n1 = ffn1_norm.reshape(depth, 1, d)
    n2 = ffn2_norm.reshape(depth, 1, d)
    for layer in range(depth):
        xs, wg, wu, wd = _ffn_stream(xs, n1, ffn1_gate, ffn1_up, ffn1_down, fw, layer)
        xp = _ffn(xp, n1, wg, wu, wd, fw, layer)
        mnw = mix_norm[layer].reshape(1, d)
        if layer % 2 == 0:
            e = layer // 2
            cw = (conv_a_w[e], conv_a_b[e].reshape(1, conv_ch), conv_a_ln_g[e].reshape(1, conv_ch),
                  conv_a_ln_b[e].reshape(1, conv_ch))
            wa, wo = ab_out_w[e][:conv_ch], ab_out_w[e][conv_ch:]
            u, qb, kt, vt, kb, vb = _ab_in(xp, mnw, ab_in_w[e], bsz=bp)
            a_out, cvp = _conv_a_prompt(u, bp, *cw)
            o = _sb_attn_prompt(qb, kb, vb, sb_beta_bias[e], bp)
            xp = _out2(xp, a_out, o, wa, wo)
            outs["kp"].append(kt)
            outs["vp"].append(vt)
            outs["cap"].append(cvp)

            u, qb, k, v, _, _ = _ab_in(xs, mnw, ab_in_w[e])
            a_out, cvs = _conv_a_sample(state_conv_a[e], u, *cw)
            o = _sb_attn_decode(qb.astype(F32), cache_kt, cache_vt, e, page_table, sb_beta_bias[e])
            xs = _out2(xs, a_out.reshape(bs, conv_ch), o.reshape(bs, hd), wa, wo)
            outs["ks"].append(k.reshape(bs, 1, SB_HEADS, SB_HEAD_DIM))
            outs["vs"].append(v.reshape(bs, 1, SB_HEADS, SB_HEAD_DIM))
            outs["cas"].append(cvs)
        else:
            o = layer // 2
            w_in = m_in_w[o]
            wz, wx, wdt = w_in[:, :d_inner], w_in[:, d_inner:d_inner + conv_dim], w_in[:, d_inner + conv_dim:]
            wdtt = wdt.T
            a = -jnp.exp(m_A_log[o])
            d_e = jnp.repeat(m_D[o], M_HEAD_DIM).reshape(1, d_inner)
            nw = m_norm[o].reshape(1, d_inner)
            cb = m_conv_b[o].reshape(1, conv_dim)

            z, dt, dtt, x, bm, cm, mcp = _m_in_conv_prompt(xp, bp, mnw, wz, wx, wdt, wdtt, m_conv_w[o], cb)
            y, hT = _ssd_prompt(x, bm, cm, z, dt, dtt, m_dt_bias[o], a, d_e, nw, bp)
            xp = _out1(xp, y, m_out_w[o])
            outs["mcp"].append(mcp)
            outs["ssp"].append(hT.reshape(bp, nh, M_HEAD_DIM, M_D_STATE))

            z, xbc, dt, _ = _m_in(xs, mnw, wz, wx, wdt, wdtt)
            y, mcs, hn = _m_step(state_mamba_conv[o], xbc, z, dt, state_ssm[o], m_conv_w[o], cb,
                                 m_dt_bias[o], a, d_e, nw)
            xs = _out1(xs, y.reshape(bs, d_inner), m_out_w[o])
            outs["mcs"].append(mcs)
            outs["sss"].append(hn.reshape(bs, nh, M_HEAD_DIM, M_D_STATE))
        last = layer == depth - 1
        xs, wg, wu, wd = _ffn_stream(xs, n2, ffn2_gate, ffn2_up, ffn2_down, fw, layer, final=last)
        xp = _ffn(xp, n2, wg, wu, wd, fw, layer, final=last)
    st = lambda key: jnp.stack(outs[key])

    def token_major(a):
        return jnp.transpose(a.reshape(a.shape[0], bp, SB_HEADS, SB_HEAD_DIM, seq), (0, 1, 4, 2, 3))

    return (xp.reshape(bp, seq, d), xs.reshape(bs, 1, d),
            token_major(st("kp")), token_major(st("vp")), st("cap"), st("mcp"), st("ssp"),
            st("ks"), st("vs"), st("cas"), st("mcs"), st("sss"))
```

```python
import functools

import jax
import jax.numpy as jnp
from jax import lax
from jax.experimental import pallas as pl
from jax.experimental.pallas import tpu as pltpu

F32 = jnp.float32
BF16 = jnp.bfloat16
EPS = 1e-6

LANES = 128
SUBLANES = 8
VMEM_LIMIT_BYTES = 56 * 1024 * 1024

SB_HEADS = 8
SB_HEAD_DIM = 64
CONV_W = 31
M_HEAD_DIM = 64
M_GROUPS = 4
M_D_STATE = 128
M_CONV_W = 4
SSD_CHUNK = 128


def _cparams(*sem):
    return pltpu.CompilerParams(dimension_semantics=sem, vmem_limit_bytes=VMEM_LIMIT_BYTES)


def _resident(shape):
    nd = len(shape)
    return pl.BlockSpec(shape, lambda *_: (0,) * nd, pipeline_mode=pl.Buffered(1))


def _resident_layer(shape, layer):
    nd = len(shape)
    return pl.BlockSpec((None,) + tuple(shape[1:]), lambda *_: (layer,) + (0,) * (nd - 1),
                        pipeline_mode=pl.Buffered(1))


def _rms_bf16(x, w):
    y = x * lax.rsqrt(jnp.mean(x * x, axis=-1, keepdims=True) + EPS)
    return (y * w).astype(BF16)


def _dot(a, b):
    return jnp.dot(a, b, preferred_element_type=F32)


def _dot_nt(a, b):
    return lax.dot_general(a, b, (((1,), (1,)), ((), ())), preferred_element_type=F32)


def _split2(v):
    hi = v.astype(BF16)
    lo = (v - hi.astype(F32)).astype(BF16)
    return hi, lo


def _split3(v):
    hi = v.astype(BF16)
    r = v - hi.astype(F32)
    mid = r.astype(BF16)
    lo = (r - mid.astype(F32)).astype(BF16)
    return hi, mid, lo


def _dot_exact_rhs(parts, m):
    acc = _dot(parts[0], m)
    for p in parts[1:]:
        acc = acc + _dot(p, m)
    return acc


def _softplus(z):
    return jnp.maximum(z, 0.0) + jnp.log1p(jnp.exp(-jnp.abs(z)))


def _silu(v):
    return v * jax.nn.sigmoid(v)


TOKEN_TILE = 512
QUERY_TILE = 256


def _token_tile(t, want):
    return want if t % want == 0 else t


def _ffn_body(x_ref, nw_ref, wg_ref, wu_ref, wd_ref, fw_ref, o_ref, *, final):
    x = x_ref[...]
    h = _rms_bf16(x, nw_ref[...])
    g = _dot(h, wg_ref[...])
    u = _dot(h, wu_ref[...])
    a = (_silu(g) * u).astype(BF16)
    y = x + 0.5 * _dot(a, wd_ref[...])
    if final:
        y = y * lax.rsqrt(jnp.mean(y * y, axis=-1, keepdims=True) + EPS) * fw_ref[...]
    o_ref[...] = y


_FFN_CHUNK = 256


def _ffn_stream_body(x_ref, nw_ref, wg_ref, wu_ref, wd_ref, fw_ref, o_ref, wgb_ref, wub_ref, wdb_ref,
                     h_ref, acc_ref, *, final):
    c = pl.program_id(0)

    @pl.when(c == 0)
    def _():
        h_ref[...] = _rms_bf16(x_ref[...], nw_ref[...])
        acc_ref[...] = jnp.zeros_like(acc_ref)

    wg, wu, wd = wg_ref[...].astype(BF16), wu_ref[...].astype(BF16), wd_ref[...].astype(BF16)
    wgb_ref[...], wub_ref[...], wdb_ref[...] = wg, wu, wd
    h = h_ref[...]
    acc_ref[...] += _dot((_silu(_dot(h, wg)) * _dot(h, wu)).astype(BF16), wd)

    @pl.when(c == pl.num_programs(0) - 1)
    def _():
        y = x_ref[...] + 0.5 * acc_ref[...]
        if final:
            y = y * lax.rsqrt(jnp.mean(y * y, axis=-1, keepdims=True) + EPS) * fw_ref[...]
        o_ref[...] = y


def _ffn_stream(x, nw, wg, wu, wd, fw, layer, *, final=False):
    t, d = x.shape
    f = wg.shape[2]
    fc = _FFN_CHUNK
    assert f % fc == 0
    whole = lambda shape: pl.BlockSpec(shape, lambda c: (0,) * len(shape))
    return pl.pallas_call(
        functools.partial(_ffn_stream_body, final=final),
        grid=(f // fc,),
        in_specs=[whole((t, d)), pl.BlockSpec((None, 1, d), lambda c: (layer, 0, 0)),
                  pl.BlockSpec((None, d, fc), lambda c: (layer, 0, c)),
                  pl.BlockSpec((None, d, fc), lambda c: (layer, 0, c)),
                  pl.BlockSpec((None, fc, d), lambda c: (layer, c, 0)),
                  whole((1, d))],
        out_specs=[whole((t, d)), pl.BlockSpec((d, fc), lambda c: (0, c)),
                   pl.BlockSpec((d, fc), lambda c: (0, c)), pl.BlockSpec((fc, d), lambda c: (c, 0))],
        out_shape=[jax.ShapeDtypeStruct((t, d), F32), jax.ShapeDtypeStruct((d, f), BF16),
                   jax.ShapeDtypeStruct((d, f), BF16), jax.ShapeDtypeStruct((f, d), BF16)],
        scratch_shapes=[pltpu.VMEM((t, d), BF16), pltpu.VMEM((t, d), F32)],
        compiler_params=_cparams("arbitrary"),
        name="ffn_stream",
    )(x, nw, wg, wu, wd, fw)


def _ffn(x, nw, wg, wu, wd, fw, layer, *, final=False):
    t, d = x.shape
    tm = _token_tile(t, TOKEN_TILE)
    return pl.pallas_call(
        functools.partial(_ffn_body, final=final),
        grid=(t // tm,),
        in_specs=[pl.BlockSpec((tm, d), lambda i: (i, 0)),
                  _resident_layer(nw.shape, layer), _resident(wg.shape),
                  _resident(wu.shape), _resident(wd.shape),
                  _resident((1, d))],
        out_specs=pl.BlockSpec((tm, d), lambda i: (i, 0)),
        out_shape=jax.ShapeDtypeStruct((t, d), F32),
        compiler_params=_cparams("parallel"),
        name="ffn",
    )(x, nw, wg, wu, wd, fw)


def _ab_in_body(x_ref, nw_ref, w_ref, u_ref, qb_ref, k_ref, v_ref, kb_ref, vb_ref, *, c, hd, key_minor):
    h = _rms_bf16(x_ref[...], nw_ref[...])
    p = _dot(h, w_ref[...])
    u_ref[...] = p[:, :c] * jax.nn.sigmoid(p[:, c:2 * c])
    qb_ref[...] = (p[:, 2 * c:2 * c + hd] * (SB_HEAD_DIM ** -0.5)).astype(BF16)
    k = p[:, 2 * c + hd:2 * c + 2 * hd]
    v = p[:, 2 * c + 2 * hd:]
    k_ref[...] = k.T if key_minor else k
    v_ref[...] = v.T if key_minor else v
    kb_ref[...] = k.astype(BF16)
    vb_ref[...] = v.astype(BF16)


def _ab_in(x, nw, w, bsz=None):
    t, d = x.shape
    hd = SB_HEADS * SB_HEAD_DIM
    c = (w.shape[1] - 3 * hd) // 2
    tm = _token_tile(t, TOKEN_TILE)
    row = lambda n: pl.BlockSpec((tm, n), lambda i: (i, 0))
    if bsz is None:
        kv_spec, kv_shape = row(hd), jax.ShapeDtypeStruct((t, hd), F32)
    else:
        assert (t // bsz) % tm == 0
        nt = t // bsz // tm
        kv_spec = pl.BlockSpec((None, hd, tm), lambda i: (i // nt, 0, i % nt))
        kv_shape = jax.ShapeDtypeStruct((bsz, hd, t // bsz), F32)
    return pl.pallas_call(
        functools.partial(_ab_in_body, c=c, hd=hd, key_minor=bsz is not None),
        grid=(t // tm,),
        in_specs=[row(d), _resident((1, d)), _resident(w.shape)],
        out_specs=[row(c), row(hd), kv_spec, kv_spec, row(hd), row(hd)],
        out_shape=[jax.ShapeDtypeStruct((t, c), F32), jax.ShapeDtypeStruct((t, hd), BF16),
                   kv_shape, kv_shape,
                   jax.ShapeDtypeStruct((t, hd), BF16), jax.ShapeDtypeStruct((t, hd), BF16)],
        compiler_params=_cparams("parallel"),
        name="ab_in",
    )(x, nw, w)


_CONV_ROWS = 32
_CONV_HALO = 32


def _layernorm_silu(cv, g, b):
    mu = jnp.mean(cv, axis=-1, keepdims=True)
    d = cv - mu
    var = jnp.mean(d * d, axis=-1, keepdims=True)
    return _silu(d * lax.rsqrt(var + EPS) * g + b)


def _conv_a_body(u_ref, w_ref, b_ref, g_ref, be_ref, a_ref, cn_ref, ext_ref, sh_ref, *, tt):
    t = pl.program_id(1)

    @pl.when(t == 0)
    def _():
        ext_ref[0:_CONV_HALO, :] = jnp.zeros((_CONV_HALO, ext_ref.shape[1]), F32)

    ext_ref[_CONV_HALO:_CONV_HALO + tt, :] = u_ref[...]
    first = _CONV_HALO - (CONV_W - 1)
    taps = [list(range(s, CONV_W, SUBLANES)) for s in range(SUBLANES)]
    rows = _CONV_HALO + tt
    for s in range(SUBLANES):
        span = tt + taps[s][-1] - s
        sh_ref[s, 0:span, :] = pltpu.roll(ext_ref[...], rows - (first + s), axis=0)[0:span]
    for r in range(tt // _CONV_ROWS):
        acc = None
        for s in range(SUBLANES):
            for j in taps[s]:
                lo = r * _CONV_ROWS + j - s
                term = w_ref[j:j + 1, :] * sh_ref[s, lo:lo + _CONV_ROWS, :]
                acc = term if acc is None else acc + term
        cv = acc + b_ref[...]
        a_ref[r * _CONV_ROWS:(r + 1) * _CONV_ROWS, :] = _layernorm_silu(cv, g_ref[...], be_ref[...]).astype(BF16)

    @pl.when(t == pl.num_programs(1) - 1)
    def _():
        cn_ref[...] = ext_ref[_CONV_HALO + tt - (CONV_W - 1):_CONV_HALO + tt, :]

    ext_ref[0:_CONV_HALO, :] = ext_ref[tt:tt + _CONV_HALO, :]


def _conv_a_prompt(u, bsz, w, b, g, be):
    t_all, c = u.shape
    t = t_all // bsz
    tt = _token_tile(t, TOKEN_TILE)
    nt = t // tt
    return pl.pallas_call(
        functools.partial(_conv_a_body, tt=tt),
        grid=(bsz, nt),
        in_specs=[pl.BlockSpec((tt, c), lambda i, j: (i * nt + j, 0)),
                  _resident(w.shape), _resident((1, c)), _resident((1, c)), _resident((1, c))],
        out_specs=[pl.BlockSpec((tt, c), lambda i, j: (i * nt + j, 0)),
                   pl.BlockSpec((None, CONV_W - 1, c), lambda i, j: (i, 0, 0))],
        out_shape=[jax.ShapeDtypeStruct((t_all, c), BF16),
                   jax.ShapeDtypeStruct((bsz, CONV_W - 1, c), F32)],
        scratch_shapes=[pltpu.VMEM((_CONV_HALO + tt, c), F32),
                        pltpu.VMEM((SUBLANES, tt + _CONV_HALO - SUBLANES, c), F32)],
        compiler_params=_cparams("parallel", "arbitrary"),
        name="conv_a_prompt",
    )(u, w, b, g, be)


def _conv_a_sample_body(prev_ref, u_ref, w_ref, b_ref, g_ref, be_ref, a_ref, cn_ref):
    nprev = CONV_W - 1
    prev = prev_ref[...]
    u = u_ref[...]
    cv = jnp.sum(w_ref[0:nprev, :] * prev, axis=0, keepdims=True) + w_ref[nprev:CONV_W, :] * u + b_ref[...]
    a_ref[...] = _layernorm_silu(cv, g_ref[...], be_ref[...])
    cn_ref[0:nprev - 1, :] = prev_ref[1:nprev, :]
    cn_ref[nprev - 1:nprev, :] = u


def _conv_a_sample(prev, u, w, b, g, be):
    bsz, nprev, c = prev.shape
    return pl.pallas_call(
        _conv_a_sample_body,
        grid=(bsz,),
        in_specs=[pl.BlockSpec((None, nprev, c), lambda i: (i, 0, 0)),
                  pl.BlockSpec((None, 1, c), lambda i: (i, 0, 0)),
                  _resident(w.shape), _resident((1, c)), _resident((1, c)), _resident((1, c))],
        out_specs=[pl.BlockSpec((None, 1, c), lambda i: (i, 0, 0)),
                   pl.BlockSpec((None, nprev, c), lambda i: (i, 0, 0))],
        out_shape=[jax.ShapeDtypeStruct((bsz, 1, c), F32),
                   jax.ShapeDtypeStruct((bsz, nprev, c), F32)],
        compiler_params=_cparams("parallel"),
        name="conv_a_sample",
    )(prev, u.reshape(bsz, 1, c), w, b, g, be)


def _suffix_matrix(tk, with_total):
    cols = tk + LANES if with_total else tk
    j = lax.broadcasted_iota(jnp.int32, (tk, cols), 0)
    s = lax.broadcasted_iota(jnp.int32, (tk, cols), 1)
    return jnp.where((j > s) | (s >= tk), 1.0, 0.0).astype(BF16)


def _neg_abs(x):
    return lax.bitcast_convert_type(lax.bitcast_convert_type(x, jnp.uint32) | jnp.uint32(0x80000000), F32)


_SB_STRIP = 32


def _sb_weights(z, bias, sufm, carry, diagonal):
    tq, tk = z.shape
    strips = [slice(r * _SB_STRIP, (r + 1) * _SB_STRIP) for r in range(tq // _SB_STRIP)]

    def admitted(r):
        rowi = lax.broadcasted_iota(jnp.int32, (_SB_STRIP, tk), 0) + r * _SB_STRIP
        coli = lax.broadcasted_iota(jnp.int32, (_SB_STRIP, tk), 1)
        return coli < rowi

    sp_parts, arg_parts, first = [], [], []
    for r, rows in enumerate(strips):
        zs = z[rows] + bias
        sp = jnp.maximum(zs, 0.0) + jnp.log(1.0 + jnp.exp(_neg_abs(zs)))
        arg_parts.append(zs - sp)
        if diagonal:
            sp = jnp.where(admitted(r), sp, 0.0)
        first.append(sp[:, 0:1])
        sp_parts.append(sp.astype(BF16))
    sums = _dot(jnp.concatenate(sp_parts, axis=0), sufm)
    w_parts = []
    for r, rows in enumerate(strips):
        cr = carry[rows]
        after = jnp.concatenate([sums[rows, c * LANES:(c + 1) * LANES] + cr for c in range(tk // LANES)], axis=1)
        wgt = jnp.exp(arg_parts[r] - after)
        if diagonal:
            wgt = jnp.where(admitted(r), wgt, 0.0)
        w_parts.append(wgt.astype(BF16))
    total = sums[:, 0:1] + jnp.concatenate(first, axis=0)
    return jnp.concatenate(w_parts, axis=0), carry + jnp.broadcast_to(total, carry.shape)


def _sb_prompt_body(bias_ref, q_ref, k_ref, v_ref, o_ref, qm_ref, *refs, tq):
    i = pl.program_id(1)
    tk = tq
    sufm = _suffix_matrix(tk, False)
    lane = lax.broadcasted_iota(jnp.int32, (1, LANES), 1)
    sels = [lane < SB_HEAD_DIM, lane >= SB_HEAD_DIM]
    pair_lanes = [slice(hp * LANES, (hp + 1) * LANES) for hp in range(SB_HEADS // 2)]
    for head in range(SB_HEADS):
        qp = q_ref[:, pair_lanes[head // 2]]
        qm_ref[head] = jnp.where(sels[head % 2], qp, jnp.zeros_like(qp))

    per_head = lambda k: refs[k * SB_HEADS:(k + 1) * SB_HEADS]
    z_refs, w_refs, carry_refs = per_head(0), per_head(1), per_head(2)
    acc_refs = refs[3 * SB_HEADS:]
    for r in carry_refs + acc_refs:
        r[...] = jnp.zeros_like(r)

    def logits(kj):
        start = pl.multiple_of(kj * tk, tk)
        for head in range(SB_HEADS):
            z_refs[head][...] = _dot_nt(qm_ref[head], k_ref[pl.ds(start, tk), pair_lanes[head // 2]])

    def weights(diagonal):
        for head in range(SB_HEADS):
            w_refs[head][...], carry_refs[head][...] = _sb_weights(
                z_refs[head][...], bias_ref[head], sufm, carry_refs[head][...], diagonal)

    def apply(kj):
        start = pl.multiple_of(kj * tk, tk)
        for hp in range(SB_HEADS // 2):
            vb = v_ref[pl.ds(start, tk), pair_lanes[hp]]
            acc_refs[hp][...] += (
                _dot(w_refs[2 * hp][...], jnp.where(sels[0], vb, jnp.zeros_like(vb)))
                + _dot(w_refs[2 * hp + 1][...], jnp.where(sels[1], vb, jnp.zeros_like(vb))))

    logits(i)
    weights(True)
    logits(jnp.maximum(i - 1, 0))

    def step(jj, _):
        kj = i - jj
        apply(kj + 1)
        weights(False)
        logits(jnp.maximum(kj - 1, 0))
        return 0

    lax.fori_loop(1, i + 1, step, 0)
    apply(0)
    for hp in range(SB_HEADS // 2):
        o_ref[:, pair_lanes[hp]] = acc_refs[hp][...].astype(o_ref.dtype)


def _sb_attn_prompt(qb, kb, vb, bias, bsz):
    t_all, hd = qb.shape
    t = t_all // bsz
    tq = _token_tile(t, QUERY_TILE)
    nq = t // tq
    return pl.pallas_call(
        functools.partial(_sb_prompt_body, tq=tq),
        grid=(bsz, nq),
        in_specs=[pl.BlockSpec(memory_space=pltpu.SMEM),
                  pl.BlockSpec((tq, hd), lambda b, i: (b * nq + i, 0)),
                  pl.BlockSpec((t, hd), lambda b, i: (b, 0)),
                  pl.BlockSpec((t, hd), lambda b, i: (b, 0))],
        out_specs=pl.BlockSpec((tq, hd), lambda b, i: (b * nq + i, 0)),
        out_shape=jax.ShapeDtypeStruct((t_all, hd), BF16),
        scratch_shapes=([pltpu.VMEM((SB_HEADS, tq, LANES), BF16)]
                        + [pltpu.VMEM((tq, tq), F32)] * SB_HEADS
                        + [pltpu.VMEM((tq, tq), BF16)] * SB_HEADS
                        + [pltpu.VMEM((tq, LANES), F32)] * SB_HEADS
                        + [pltpu.VMEM((tq, LANES), F32)] * (SB_HEADS // 2)),
        compiler_params=_cparams("parallel", "arbitrary"),
        name="sb_attn_prompt",
    )(bias, qb, kb, vb)


_DEC_PAGES = 32


def _sb_decode_body(pt_ref, bias_ref, q_ref, *refs, page, npg):
    k_refs, v_refs = refs[:npg], refs[npg:2 * npg]
    o_ref, acc_ref, carry_ref = refs[2 * npg:]
    p = pl.program_id(1)
    hd = SB_HEADS * SB_HEAD_DIM

    @pl.when(p == 0)
    def _():
        acc_ref[...] = jnp.zeros_like(acc_ref)
        carry_ref[...] = jnp.zeros_like(carry_ref)

    own = (lax.broadcasted_iota(jnp.int32, (SB_HEADS, hd), 1) // SB_HEAD_DIM
           == lax.broadcasted_iota(jnp.int32, (SB_HEADS, hd), 0))
    q8 = jnp.where(own, jnp.broadcast_to(q_ref[...], (SB_HEADS, hd)), 0.0).astype(BF16)
    sufm = _suffix_matrix(page, True)
    args, pieces = [], []
    for g in range(npg):
        kt = k_refs[g][...].reshape(hd, page).astype(BF16)
        z = _dot(q8, kt) + bias_ref[...]
        sp = _softplus(z)
        args.append(z - sp)
        pieces.append(sp)
    sums = _dot_exact_rhs(_split2(jnp.concatenate(pieces, axis=0)), sufm)
    acc = acc_ref[...]
    run = carry_ref[...]
    for g in range(npg):
        rows = slice(g * SB_HEADS, (g + 1) * SB_HEADS)
        wgt = jnp.exp(args[g] - (sums[rows, :page] + run))
        run = run + sums[rows, page:]
        vt = v_refs[g][...].reshape(hd, page).astype(BF16)
        acc = acc + _dot_nt(wgt.astype(BF16), vt)
    acc_ref[...] = acc
    carry_ref[...] = run

    @pl.when(p == pl.num_programs(1) - 1)
    def _():
        o_ref[...] = jnp.sum(jnp.where(own, acc, 0.0), axis=0, keepdims=True)


def _sb_attn_decode(q, cache_kt, cache_vt, layer, page_table, bias):
    bsz, hd = q.shape
    n_pages = page_table.shape[1]
    page = cache_kt.shape[4]
    assert page == LANES
    npg = _DEC_PAGES if n_pages % _DEC_PAGES == 0 else 1

    def page_spec(g):
        return pl.BlockSpec((None, None, SB_HEADS, SB_HEAD_DIM, page),
                            lambda b, p, pt: (layer, pt[b * n_pages + n_pages - 1 - (p * npg + g)], 0, 0, 0))

    per_b = pl.BlockSpec((None, 1, hd), lambda b, p, pt: (b, 0, 0))
    grid_spec = pltpu.PrefetchScalarGridSpec(
        num_scalar_prefetch=1,
        grid=(bsz, n_pages // npg),
        in_specs=[pl.BlockSpec((SB_HEADS, 1), lambda b, p, pt: (0, 0)), per_b]
                 + [page_spec(g) for g in range(npg)] * 2,
        out_specs=per_b,
        scratch_shapes=[pltpu.VMEM((SB_HEADS, hd), F32), pltpu.VMEM((SB_HEADS, LANES), F32)],
    )
    return pl.pallas_call(
        functools.partial(_sb_decode_body, page=page, npg=npg),
        grid_spec=grid_spec,
        out_shape=jax.ShapeDtypeStruct((bsz, 1, hd), F32),
        compiler_params=_cparams("parallel", "arbitrary"),
        name="sb_attn_decode",
    )(page_table.reshape(-1), bias.reshape(SB_HEADS, 1), q.reshape(bsz, 1, hd),
      *([cache_kt] * npg), *([cache_vt] * npg))


def _out2_body(x_ref, a_ref, o_ref, wa_ref, wo_ref, y_ref):
    y_ref[...] = (x_ref[...] + _dot(a_ref[...].astype(BF16), wa_ref[...])
                  + _dot(o_ref[...].astype(BF16), wo_ref[...]))


def _out2(x, a, o, wa, wo):
    t, d = x.shape
    tm = _token_tile(t, TOKEN_TILE)
    row = lambda n: pl.BlockSpec((tm, n), lambda i: (i, 0))
    return pl.pallas_call(
        _out2_body,
        grid=(t // tm,),
        in_specs=[row(d), row(a.shape[1]), row(o.shape[1]), _resident(wa.shape), _resident(wo.shape)],
        out_specs=row(d),
        out_shape=jax.ShapeDtypeStruct((t, d), F32),
        compiler_params=_cparams("parallel"),
        name="ab_out",
    )(x, a, o, wa, wo)


def _out1_body(x_ref, y_ref, w_ref, o_ref):
    o_ref[...] = x_ref[...] + _dot(y_ref[...].astype(BF16), w_ref[...])


def _out1(x, y, w):
    t, d = x.shape
    tm = _token_tile(t, TOKEN_TILE)
    row = lambda n: pl.BlockSpec((tm, n), lambda i: (i, 0))
    return pl.pallas_call(
        _out1_body,
        grid=(t // tm,),
        in_specs=[row(d), row(y.shape[1]), _resident(w.shape)],
        out_specs=row(d),
        out_shape=jax.ShapeDtypeStruct((t, d), F32),
        compiler_params=_cparams("parallel"),
        name="m_out",
    )(x, y, w)


def _m_in_body(x_ref, nw_ref, w_ref, wdtt_ref, z_ref, xbc_ref, dt_ref, dtt_ref, *, d_inner, cdim):
    h = _rms_bf16(x_ref[...], nw_ref[...])
    z_ref[...] = _dot(h, w_ref[:, :d_inner])
    xbc_ref[...] = _dot(h, w_ref[:, d_inner:d_inner + cdim])
    dt_ref[...] = _dot(h, w_ref[:, d_inner + cdim:])
    dtt_ref[...] = _dot_nt(wdtt_ref[...], h)


def _m_in(x, nw, w, wdtt, layer, d_inner, cdim):
    t, d = x.shape
    tm = _token_tile(t, TOKEN_TILE)
    nh = wdtt.shape[0]
    row = lambda n: pl.BlockSpec((tm, n), lambda i: (i, 0))
    return pl.pallas_call(
        functools.partial(_m_in_body, d_inner=d_inner, cdim=cdim),
        grid=(t // tm,),
        in_specs=[row(d), _resident((1, d)), _resident_layer(w.shape, layer), _resident(wdtt.shape)],
        out_specs=[row(d_inner), row(cdim), row(nh), pl.BlockSpec((nh, tm), lambda i: (0, i))],
        out_shape=[jax.ShapeDtypeStruct((t, d_inner), F32), jax.ShapeDtypeStruct((t, cdim), F32),
                   jax.ShapeDtypeStruct((t, nh), F32), jax.ShapeDtypeStruct((nh, t), F32)],
        compiler_params=_cparams("parallel"),
        name="m_in",
    )(x, nw, w, wdtt)


_MCONV_LANES = 512


def _m_conv_tile(src_ref, lane_chunks, w_ref, b_ref, x_ref, bm_ref, cm_ref, *, tt, d_inner, gn):
    halo = SUBLANES
    nprev = M_CONV_W - 1
    first = halo - nprev
    for lc in lane_chunks:
        ls = slice(lc * _MCONV_LANES, (lc + 1) * _MCONV_LANES)
        for r in range(tt // _CONV_ROWS):
            span = _CONV_ROWS + halo
            win = src_ref[r * _CONV_ROWS:r * _CONV_ROWS + span, ls]
            acc = w_ref[nprev:M_CONV_W, ls] * win[halo:span]
            for j in range(nprev):
                acc = acc + w_ref[j:j + 1, ls] * pltpu.roll(win, span - (first + j), axis=0)[0:_CONV_ROWS]
            val = _silu(acc + b_ref[:, ls])
            rows = slice(r * _CONV_ROWS, (r + 1) * _CONV_ROWS)
            lo = lc * _MCONV_LANES
            if lo < d_inner:
                x_ref[rows, lo:lo + _MCONV_LANES] = val
            elif lo < d_inner + gn:
                bm_ref[rows, lo - d_inner:lo - d_inner + _MCONV_LANES] = val
            else:
                cm_ref[rows, lo - d_inner - gn:lo - d_inner - gn + _MCONV_LANES] = val.astype(BF16)


def _m_in_conv_body(x_ref, nw_ref, w_ref, wdtt_ref, cw_ref, cb_ref,
                    z_ref, dt_ref, dtt_ref, xc_ref, bm_ref, cm_ref, cn_ref, buf0_ref, buf1_ref,
                    *, tm, nt, d_inner, gn):
    j = pl.program_id(1)
    halo = SUBLANES
    bufs = (buf0_ref, buf1_ref)
    cdim = buf0_ref.shape[1]
    n_chunks = cdim // _MCONV_LANES
    conv = functools.partial(_m_conv_tile, w_ref=cw_ref, b_ref=cb_ref, x_ref=xc_ref, bm_ref=bm_ref,
                             cm_ref=cm_ref, tt=tm, d_inner=d_inner, gn=gn)

    @pl.when(j == 0)
    def _():
        buf1_ref[...] = jnp.zeros_like(buf1_ref)

    for parity in range(2):
        cur, prev = bufs[parity], bufs[1 - parity]

        @pl.when((j < nt) & (j % 2 == parity))
        def _(cur=cur, prev=prev):
            cur[0:halo, :] = prev[tm:tm + halo, :]
            h = _rms_bf16(x_ref[...], nw_ref[...])
            for lc in range(n_chunks):
                ls = slice(lc * _MCONV_LANES, (lc + 1) * _MCONV_LANES)
                cur[halo:halo + tm, ls] = _dot(h, w_ref[:, d_inner + ls.start:d_inner + ls.stop])
                conv(prev, [lc])
            z_ref[...] = _dot(h, w_ref[:, :d_inner])
            dt_ref[...] = _dot(h, w_ref[:, d_inner + cdim:])
            dtt_ref[...] = _dot_nt(wdtt_ref[...], h)

    last = bufs[(nt - 1) % 2]

    @pl.when(j == nt)
    def _():
        conv(last, range(n_chunks))
        cn_ref[...] = last[halo + tm - (M_CONV_W - 1):halo + tm, :]


def _m_in_conv_prompt(x, bsz, nw, w, wdtt, layer, d_inner, cw, cb):
    t_all, d = x.shape
    cdim, nh = cw.shape[1], wdtt.shape[0]
    gn = M_GROUPS * M_D_STATE
    assert gn == _MCONV_LANES and d_inner % _MCONV_LANES == 0 and cdim == d_inner + 2 * gn
    t = t_all // bsz
    tm = _token_tile(t, TOKEN_TILE)
    nt = t // tm
    proj = lambda n: pl.BlockSpec((tm, n), lambda i, j: (i * nt + jnp.minimum(j, nt - 1), 0))
    conv = lambda n: pl.BlockSpec((tm, n), lambda i, j: (i * nt + jnp.maximum(j - 1, 0), 0))
    return pl.pallas_call(
        functools.partial(_m_in_conv_body, tm=tm, nt=nt, d_inner=d_inner, gn=gn),
        grid=(bsz, nt + 1),
        in_specs=[proj(d), _resident((1, d)), _resident_layer(w.shape, layer), _resident(wdtt.shape),
                  _resident(cw.shape), _resident((1, cdim))],
        out_specs=[proj(d_inner), proj(nh),
                   pl.BlockSpec((nh, tm), lambda i, j: (0, i * nt + jnp.minimum(j, nt - 1))),
                   conv(d_inner), conv(gn), conv(gn),
                   pl.BlockSpec((None, M_CONV_W - 1, cdim), lambda i, j: (i, 0, 0))],
        out_shape=[jax.ShapeDtypeStruct((t_all, d_inner), F32), jax.ShapeDtypeStruct((t_all, nh), F32),
                   jax.ShapeDtypeStruct((nh, t_all), F32),
                   jax.ShapeDtypeStruct((t_all, d_inner), F32), jax.ShapeDtypeStruct((t_all, gn), F32),
                   jax.ShapeDtypeStruct((t_all, gn), BF16),
                   jax.ShapeDtypeStruct((bsz, M_CONV_W - 1, cdim), F32)],
        scratch_shapes=[pltpu.VMEM((SUBLANES + tm, cdim), F32)] * 2,
        compiler_params=_cparams("parallel", "arbitrary"),
        name="m_in_conv_prompt",
    )(x, nw, w, wdtt, cw, cb)


_EXPAND_PIECES = 3


def _head_expand_matrix(nh, width):
    h = lax.broadcasted_iota(jnp.int32, (_EXPAND_PIECES * nh, nh * width), 0) % nh
    c = lax.broadcasted_iota(jnp.int32, (_EXPAND_PIECES * nh, nh * width), 1)
    return jnp.where(c // width == h, 1.0, 0.0).astype(BF16)


def _expand_heads(v, expand):
    return _dot(jnp.concatenate(_split3(v), axis=1), expand)


def _gate_norm(y, z, nw, d_inner):
    y = y * _silu(z)
    gw = d_inner // M_GROUPS
    parts = []
    for g in range(M_GROUPS):
        yg = y[:, g * gw:(g + 1) * gw]
        parts.append(yg * lax.rsqrt(jnp.mean(yg * yg, axis=-1, keepdims=True) + EPS))
    return jnp.concatenate(parts, axis=1) * nw


def _ssd_body(x_ref, bm_ref, cm_ref, z_ref, dt_ref, dtt_ref, dtb_ref, dtbt_ref, a_ref, at_ref, de_ref, nw_ref,
              y_ref, hT_ref, h_ref, yd_ref, *, nh, d_inner):
    c = pl.program_id(1)
    L = SSD_CHUNK
    hpg = nh // M_GROUPS
    gw = hpg * M_HEAD_DIM

    @pl.when(c == 0)
    def _():
        h_ref[...] = jnp.zeros_like(h_ref)

    dt = _softplus(dt_ref[...] + dtb_ref[...])
    dtt = _softplus(dtt_ref[...] + dtbt_ref[...])
    row = lax.broadcasted_iota(jnp.int32, (L, L), 0)
    col = lax.broadcasted_iota(jnp.int32, (L, L), 1)
    causal = col <= row
    tri = jnp.where(causal, 1.0, 0.0).astype(BF16)
    trit = jnp.where(row <= col, 1.0, 0.0).astype(BF16)
    dta_hi, dta_lo = _split2(dt * a_ref[...])
    acs = _dot(tri, dta_hi) + _dot(tri, dta_lo)
    acst = _dot_exact_rhs(_split2(dtt * at_ref[...]), trit)
    acs_last = acs[L - 1:L, :]
    expand = _head_expand_matrix(nh, M_HEAD_DIM)
    dt_e = _expand_heads(dt, expand)
    eacs_e = _expand_heads(jnp.exp(acs), expand)
    eend_e = _expand_heads(jnp.exp(acs_last - acs), expand)
    etot_e = _expand_heads(jnp.broadcast_to(jnp.exp(acs_last), (SUBLANES, nh)), expand)[0:1, :]

    x = x_ref[...]
    dtx = x * dt_e
    dtx_b = dtx.astype(BF16)
    dtxe_b = (dtx * eend_e).astype(BF16)
    lane = lax.broadcasted_iota(jnp.int32, (1, LANES), 1)
    assert 2 * M_HEAD_DIM == LANES and hpg % 2 == 0
    for g in range(M_GROUPS):
        ns = slice(g * M_D_STATE, (g + 1) * M_D_STATE)
        gs = slice(g * gw, (g + 1) * gw)
        bg = bm_ref[:, ns]
        cg = cm_ref[:, ns]
        cb = _dot_nt(cg, bg.astype(BF16))
        hg = h_ref[:, gs]
        yd_ref[:, gs] = _dot(cg, hg.astype(BF16)) * eacs_e[:, gs]
        h_ref[:, gs] = hg * etot_e[:, gs] + _dot(bg.T.astype(BF16), dtxe_b[:, gs])
        for hp in range(hpg // 2):
            pair = g * hpg + 2 * hp
            ps = slice(pair * M_HEAD_DIM, (pair + 2) * M_HEAD_DIM)
            ms = []
            for head in (pair, pair + 1):
                dec = jnp.exp(jnp.where(causal, acs[:, head:head + 1] - acst[head:head + 1, :], -jnp.inf))
                ms.append((cb * dec).astype(BF16))
            dp = dtx_b[:, ps]
            rhs = jnp.concatenate([jnp.where(lane < M_HEAD_DIM, dp, jnp.zeros_like(dp)),
                                   jnp.where(lane >= M_HEAD_DIM, dp, jnp.zeros_like(dp))], axis=0)
            yd_ref[:, ps] += _dot(jnp.concatenate(ms, axis=1), rhs)

    y = yd_ref[...] + de_ref[...] * x
    y_ref[...] = _gate_norm(y, z_ref[...], nw_ref[...], d_inner).astype(BF16)

    @pl.when(c == pl.num_programs(1) - 1)
    def _():
        hT_ref[...] = h_ref[...].T


def _ssd_prompt(x, bm, cm, z, dt, dtt, dtb, a, d_e, nw, bsz):
    t_all, d_inner = x.shape
    nh = dt.shape[1]
    gn = bm.shape[1]
    t = t_all // bsz
    L = SSD_CHUNK
    assert t % L == 0
    nc = t // L
    row = lambda n: pl.BlockSpec((L, n), lambda b, c: (b * nc + c, 0))
    return pl.pallas_call(
        functools.partial(_ssd_body, nh=nh, d_inner=d_inner),
        grid=(bsz, nc),
        in_specs=[row(d_inner), row(gn), row(gn), row(d_inner), row(nh),
                  pl.BlockSpec((nh, L), lambda b, c: (0, b * nc + c)),
                  _resident((1, nh)), _resident((nh, 1)), _resident((1, nh)), _resident((nh, 1)),
                  _resident((1, d_inner)), _resident((1, d_inner))],
        out_specs=[row(d_inner), pl.BlockSpec((None, d_inner, M_D_STATE), lambda b, c: (b, 0, 0))],
        out_shape=[jax.ShapeDtypeStruct((t_all, d_inner), BF16),
                   jax.ShapeDtypeStruct((bsz, d_inner, M_D_STATE), F32)],
        scratch_shapes=[pltpu.VMEM((M_D_STATE, d_inner), F32), pltpu.VMEM((L, d_inner), F32)],
        compiler_params=_cparams("parallel", "arbitrary"),
        name="ssd_prompt",
    )(x, bm, cm, z, dt, dtt, dtb.reshape(1, nh), dtb.reshape(nh, 1), a.reshape(1, nh), a.reshape(nh, 1), d_e, nw)


def _m_step_body(cprev_ref, xbc_ref, z_ref, dt_ref, st_ref, w_ref, b_ref, dtb_ref, a_ref, de_ref, nw_ref,
                 y_ref, cn_ref, hn_ref, *, nh, d_inner):
    nprev = M_CONV_W - 1
    gn = M_GROUPS * M_D_STATE
    gw = d_inner // M_GROUPS
    new = xbc_ref[...]
    acc = w_ref[nprev:M_CONV_W, :] * new + b_ref[...]
    for j in range(nprev):
        acc = acc + w_ref[j:j + 1, :] * cprev_ref[j:j + 1, :]
    cn_ref[0:nprev - 1, :] = cprev_ref[1:nprev, :]
    cn_ref[nprev - 1:nprev, :] = new
    xbc = _silu(acc)
    x = xbc[:, :d_inner]

    dt = _softplus(dt_ref[...] + dtb_ref[...])
    expand = _head_expand_matrix(nh, M_HEAD_DIM)
    rows8 = lambda v: jnp.broadcast_to(v, (SUBLANES, nh))
    dt_e = _expand_heads(rows8(dt), expand)[0:1, :]
    da_e = _expand_heads(rows8(jnp.exp(dt * a_ref[...])), expand)[0:1, :]
    dtx = x * dt_e

    rowi = lax.broadcasted_iota(jnp.int32, (M_D_STATE, M_D_STATE), 0)
    b_mat = jnp.zeros((M_D_STATE, M_D_STATE), F32)
    c_mat = jnp.zeros((M_D_STATE, M_D_STATE), F32)
    for g in range(M_GROUPS):
        b_row = xbc[:, d_inner + g * M_D_STATE:d_inner + (g + 1) * M_D_STATE]
        c_row = xbc[:, d_inner + gn + g * M_D_STATE:d_inner + gn + (g + 1) * M_D_STATE]
        b_mat = jnp.where(rowi == g, b_row, b_mat)
        c_mat = jnp.where(rowi == g, c_row, c_mat)
    b_cols = b_mat.T
    c_cols = c_mat.T

    ht = st_ref[...].T
    hn_parts, y_parts = [], []
    for g in range(M_GROUPS):
        gs = slice(g * gw, (g + 1) * gw)
        hn = ht[:, gs] * da_e[:, gs] + b_cols[:, g:g + 1] * dtx[:, gs]
        hn_parts.append(hn)
        y_parts.append(jnp.sum(c_cols[:, g:g + 1] * hn, axis=0, keepdims=True))
    hn_ref[...] = jnp.concatenate(hn_parts, axis=1).T
    y = jnp.concatenate(y_parts, axis=1) + de_ref[...] * x
    y_ref[...] = _gate_norm(y, z_ref[...], nw_ref[...], d_inner)


def _m_step(conv_prev, xbc, z, dt, state, w, b, dtb, a, d_e, nw):
    bsz, nprev, cdim = conv_prev.shape
    d_inner = z.shape[1]
    nh = dt.shape[1]
    per_b = lambda r, n: pl.BlockSpec((None, r, n), lambda i: (i, 0, 0))
    return pl.pallas_call(
        functools.partial(_m_step_body, nh=nh, d_inner=d_inner),
        grid=(bsz,),
        in_specs=[per_b(nprev, cdim), per_b(1, cdim), per_b(1, d_inner), per_b(1, nh), per_b(d_inner, M_D_STATE),
                  _resident(w.shape), _resident((1, cdim)), _resident((1, nh)), _resident((1, nh)),
                  _resident((1, d_inner)), _resident((1, d_inner))],
        out_specs=[per_b(1, d_inner), per_b(nprev, cdim), per_b(d_inner, M_D_STATE)],
        out_shape=[jax.ShapeDtypeStruct((bsz, 1, d_inner), F32),
                   jax.ShapeDtypeStruct((bsz, nprev, cdim), F32),
                   jax.ShapeDtypeStruct((bsz, d_inner, M_D_STATE), F32)],
        compiler_params=_cparams("parallel"),
        name="m_step",
    )(conv_prev, xbc.reshape(bsz, 1, cdim), z.reshape(bsz, 1, d_inner), dt.reshape(bsz, 1, nh),
      state.reshape(bsz, d_inner, M_D_STATE), w, b, dtb.reshape(1, nh), a.reshape(1, nh), d_e, nw)


def kernel(x_prompt, x_sample, cache_sb_k, cache_sb_v, state_conv_a, state_mamba_conv, state_ssm, page_table,
           ffn1_norm, ffn1_gate, ffn1_up, ffn1_down, mix_norm, ffn2_norm, ffn2_gate, ffn2_up, ffn2_down,
           final_norm, ab_in_proj, conv_a_w, conv_a_b, conv_a_ln_g, conv_a_ln_b, sb_beta_bias, ab_out_proj,
           m_in_proj, m_conv_w, m_conv_b, m_dt_bias, m_A_log, m_D, m_norm, m_out_proj):
    bp, seq, d = x_prompt.shape
    bs = x_sample.shape[0]
    depth = ffn1_norm.shape[0]
    hd = SB_HEADS * SB_HEAD_DIM
    conv_ch = conv_a_w.shape[2]
    nh = m_dt_bias.shape[1]
    d_inner = nh * M_HEAD_DIM
    conv_dim = m_conv_w.shape[2]

    bf = lambda w: w.astype(BF16)
    ab_in_w, ab_out_w = bf(ab_in_proj), bf(ab_out_proj)
    m_in_w, m_out_w = bf(m_in_proj), bf(m_out_proj)
    fw = final_norm.reshape(1, d)
    cache_kt = jnp.transpose(cache_sb_k, (0, 1, 3, 4, 2))
    cache_vt = jnp.transpose(cache_sb_v, (0, 1, 3, 4, 2))

    xp = x_prompt.reshape(bp * seq, d)
    xs = x_sample.reshape(bs, d)
    outs = {k: [] for k in ("kp", "vp", "cap", "mcp", "ssp", "ks", "vs", "cas", "mcs", "sss")}
    n1 = ffn1_norm.reshape(depth, 1, d)
    n2 = ffn2_norm.reshape(depth, 1, d)
    for layer in range(depth):
        xs, wg, wu, wd = _ffn_stream(xs, n1, ffn1_gate, ffn1_up, ffn1_down, fw, layer)
        xp = _ffn(xp, n1, wg, wu, wd, fw, layer)
        mnw = mix_norm[layer].reshape(1, d)
        if layer % 2 == 0:
            e = layer // 2
            cw = (conv_a_w[e], conv_a_b[e].reshape(1, conv_ch), conv_a_ln_g[e].reshape(1, conv_ch),
                  conv_a_ln_b[e].reshape(1, conv_ch))
            wa, wo = ab_out_w[e][:conv_ch], ab_out_w[e][conv_ch:]
            u, qb, kt, vt, kb, vb = _ab_in(xp, mnw, ab_in_w[e], bsz=bp)
            a_out, cvp = _conv_a_prompt(u, bp, *cw)
            o = _sb_attn_prompt(qb, kb, vb, sb_beta_bias[e], bp)
            xp = _out2(xp, a_out, o, wa, wo)
            outs["kp"].append(kt)
            outs["vp"].append(vt)
            outs["cap"].append(cvp)

            u, qb, k, v, _, _ = _ab_in(xs, mnw, ab_in_w[e])
            a_out, cvs = _conv_a_sample(state_conv_a[e], u, *cw)
            o = _sb_attn_decode(qb.astype(F32), cache_kt, cache_vt, e, page_table, sb_beta_bias[e])
            xs = _out2(xs, a_out.reshape(bs, conv_ch), o.reshape(bs, hd), wa, wo)
            outs["ks"].append(k.reshape(bs, 1, SB_HEADS, SB_HEAD_DIM))
            outs["vs"].append(v.reshape(bs, 1, SB_HEADS, SB_HEAD_DIM))
            outs["cas"].append(cvs)
        else:
            o = layer // 2
            wdtt = m_in_w[o][:, d_inner + conv_dim:].T
            a = -jnp.exp(m_A_log[o])
            d_e = jnp.repeat(m_D[o], M_HEAD_DIM).reshape(1, d_inner)
            nw = m_norm[o].reshape(1, d_inner)
            cb = m_conv_b[o].reshape(1, conv_dim)

            z, dt, dtt, x, bm, cm, mcp = _m_in_conv_prompt(xp, bp, mnw, m_in_w, wdtt, o, d_inner,
                                                           m_conv_w[o], cb)
            y, hT = _ssd_prompt(x, bm, cm, z, dt, dtt, m_dt_bias[o], a, d_e, nw, bp)
            xp = _out1(xp, y, m_out_w[o])
            outs["mcp"].append(mcp)
            outs["ssp"].append(hT.reshape(bp, nh, M_HEAD_DIM, M_D_STATE))

            z, xbc, dt, _ = _m_in(xs, mnw, m_in_w, wdtt, o, d_inner, conv_dim)
            y, mcs, hn = _m_step(state_mamba_conv[o], xbc, z, dt, state_ssm[o], m_conv_w[o], cb,
                                 m_dt_bias[o], a, d_e, nw)
            xs = _out1(xs, y.reshape(bs, d_inner), m_out_w[o])
            outs["mcs"].append(mcs)
            outs["sss"].append(hn.reshape(bs, nh, M_HEAD_DIM, M_D_STATE))
        last = layer == depth - 1
        xs, wg, wu, wd = _ffn_stream(xs, n2, ffn2_gate, ffn2_up, ffn2_down, fw, layer, final=last)
        xp = _ffn(xp, n2, wg, wu, wd, fw, layer, final=last)
    st = lambda key: jnp.stack(outs[key])

    def token_major(a):
        return jnp.transpose(a.reshape(a.shape[0], bp, SB_HEADS, SB_HEAD_DIM, seq), (0, 1, 4, 2, 3))

    return (xp.reshape(bp, seq, d), xs.reshape(bs, 1, d),
            token_major(st("kp")), token_major(st("vp")), st("cap"), st("mcp"), st("ssp"),
            st("ks"), st("vs"), st("cas"), st("mcs"), st("sss"))
```
